```python
import jax, jax.numpy as jnp
from jax import lax
import numpy as np

D_MODEL = 2048
BATCH = 4
SEQ = 8192
DEPTH = 4

CHUNK = 64
N_EVEN = (DEPTH + 1) // 2
N_ODD = DEPTH // 2

POOL_WIDTH = D_MODEL // 2
POOL_WINDOWS = (2, 4, 8, 16)
POOL_GROUP = POOL_WIDTH // len(POOL_WINDOWS)
GLA_HEADS = 4
GLA_DV = (D_MODEL // 2) // GLA_HEADS
GLA_DK = GLA_DV // 2
GLA_GATE_RANK = 16
GLA_TAU = 16.0
OFF_Q = POOL_WIDTH
OFF_K = OFF_Q + GLA_HEADS * GLA_DK
OFF_V = OFF_K + GLA_HEADS * GLA_DK
OFF_R = OFF_V + GLA_HEADS * GLA_DV
OFF_LR = OFF_R + GLA_HEADS * GLA_DV
P_AB = OFF_LR + GLA_GATE_RANK
MIX_AB = POOL_WIDTH + GLA_HEADS * GLA_DV
ATT_HEADS = 16
ATT_HEAD_DIM = D_MODEL // ATT_HEADS
ATT_PAST_CHUNKS = 8
REL_CLIP = 256
FFN_DIM = 5504
CONV_WIDTH = 3
ALPHA = (2.0 * DEPTH) ** 0.25
BETA = (8.0 * DEPTH) ** -0.25
LN_EPS = 1e-5

kernel_name = "hybrid_pool_gla_chunkattn_deepnorm_adaln"


def layer_norm(x, g, b):
    xf = x.astype(jnp.float32)
    mu = jnp.mean(xf, axis=-1, keepdims=True)
    var = jnp.mean(jnp.square(xf - mu), axis=-1, keepdims=True)
    return ((xf - mu) * lax.rsqrt(var + LN_EPS)).astype(x.dtype) * g + b


def multiscale_pool(u, w_pool, pool_scale):
    S_ = u.shape[1]
    uf = u.astype(jnp.float32)
    cs = jnp.cumsum(uf, axis=1)
    t = jnp.arange(1, S_ + 1, dtype=jnp.float32)
    outs = []
    for gi, w in enumerate(POOL_WINDOWS):
        sl = slice(gi * POOL_GROUP, (gi + 1) * POOL_GROUP)
        cs_g = cs[:, :, sl]
        lagged = jnp.pad(cs_g[:, :S_ - w, :], ((0, 0), (w, 0), (0, 0)))
        count = jnp.minimum(t, float(w))[None, :, None]
        d = (cs_g - lagged) / count - uf[:, :, sl]
        outs.append(jnp.einsum('bsc,cd->bsd', d.astype(u.dtype), w_pool[gi]))
    return jnp.concatenate(outs, axis=-1) * pool_scale


def gla_chunked(q, k, v, log_a):
    B_, S_, H, DK = q.shape
    DV = v.shape[-1]
    nc = S_ // CHUNK

    def blocks(t):
        return t.astype(jnp.float32).reshape(B_, nc, CHUNK, H, t.shape[-1]).transpose(1, 0, 3, 2, 4)

    qb = blocks(q) * (DK ** -0.5)
    kb, vb, gb = blocks(k), blocks(v), blocks(log_a)
    b = jnp.cumsum(gb, axis=3)
    b_last = b[:, :, :, -1:, :]
    q_dec = qb * jnp.exp(b)
    k_inv = kb * jnp.exp(-b)
    k_dec = kb * jnp.exp(b_last - b)
    causal = jnp.tril(jnp.ones((CHUNK, CHUNK), dtype=bool))
    scores = jnp.where(causal, jnp.einsum('nbhid,nbhjd->nbhij', q_dec, k_inv), 0.0)
    o_intra = jnp.einsum('nbhij,nbhjv->nbhiv', scores, vb)
    decay = jnp.exp(b_last)

    def step(state, inp):
        q_n, k_n, v_n, decay_n = inp
        o_inter = jnp.einsum('bhid,bhdv->bhiv', q_n, state)
        state = decay_n[:, :, 0, :, None] * state + jnp.einsum('bhjd,bhjv->bhdv', k_n, v_n)
        return state, o_inter

    state0 = jnp.zeros((B_, H, DK, DV), jnp.float32)
    _, o_inter = lax.scan(step, state0, (q_dec, k_dec, vb, decay))
    o = o_intra + o_inter
    return o.transpose(1, 0, 3, 2, 4).reshape(B_, S_, H, DV)


def even_mixer(u, w_in, w_gate_lr, b_gate, gla_norm_g, w_pool, pool_scale, w_out):
    B_, S_, _ = u.shape
    z = u @ w_in
    q = z[..., OFF_Q:OFF_K].reshape(B_, S_, GLA_HEADS, GLA_DK)
    k = z[..., OFF_K:OFF_V].reshape(B_, S_, GLA_HEADS, GLA_DK)
    v = z[..., OFF_V:OFF_R].reshape(B_, S_, GLA_HEADS, GLA_DV)
    r = z[..., OFF_R:OFF_LR]
    gate_pre = z[..., OFF_LR:] @ w_gate_lr + b_gate
    log_a = (jax.nn.log_sigmoid(gate_pre.astype(jnp.float32)) / GLA_TAU).reshape(B_, S_, GLA_HEADS, GLA_DK)
    o = gla_chunked(q, k, v, log_a)
    mu = jnp.mean(o, axis=-1, keepdims=True)
    var = jnp.mean(jnp.square(o - mu), axis=-1, keepdims=True)
    o = ((o - mu) * lax.rsqrt(var + LN_EPS)).reshape(B_, S_, GLA_HEADS * GLA_DV).astype(u.dtype)
    y_b = jax.nn.silu(r) * (o * gla_norm_g)
    y_a = multiscale_pool(z[..., :POOL_WIDTH], w_pool, pool_scale)
    return jnp.concatenate([y_a, y_b], axis=-1) @ w_out


def chunk_band_attention(q, k, v, rel_bias):
    B_, S_, H, dh = q.shape
    nc = S_ // CHUNK
    pad = ATT_PAST_CHUNKS * CHUNK
    band = pad + CHUNK
    kp = jnp.pad(k, ((0, 0), (pad, 0), (0, 0), (0, 0)))
    vp = jnp.pad(v, ((0, 0), (pad, 0), (0, 0), (0, 0)))
    qi = jnp.arange(CHUNK)[:, None]
    kj = jnp.arange(band)[None, :]
    rel = jnp.clip(pad + qi - kj, -REL_CLIP, REL_CLIP) + REL_CLIP
    bias = rel_bias[:, rel].astype(jnp.float32)
    scale = dh ** -0.5

    def one_chunk(n):
        start = n * CHUNK
        q_n = lax.dynamic_slice_in_dim(q, start, CHUNK, axis=1)
        k_n = lax.dynamic_slice_in_dim(kp, start, band, axis=1)
        v_n = lax.dynamic_slice_in_dim(vp, start, band, axis=1)
        s = jnp.einsum('blhd,bjhd->bhlj', q_n, k_n).astype(jnp.float32) * scale + bias
        s = jnp.where(kj >= pad - start, s, -1e30)
        p = jax.nn.softmax(s, axis=-1).astype(v.dtype)
        return jnp.einsum('bhlj,bjhd->blhd', p, v_n)

    out = lax.map(one_chunk, jnp.arange(nc))
    return out.transpose(1, 0, 2, 3, 4).reshape(B_, S_, H * dh)


def odd_mixer(u, w_qkv, rel_bias, w_o):
    B_, S_, _ = u.shape
    qkv = (u @ w_qkv).reshape(B_, S_, 3, ATT_HEADS, ATT_HEAD_DIM)
    attn = chunk_band_attention(qkv[:, :, 0], qkv[:, :, 1], qkv[:, :, 2], rel_bias)
    return attn @ w_o


def conv_ffn(u, w_up, conv_w, conv_b, w_down):
    S_ = u.shape[1]
    h = u @ w_up
    a, g = h[..., :FFN_DIM], h[..., FFN_DIM:]
    gp = jnp.pad(g, ((0, 0), (CONV_WIDTH - 1, 0), (0, 0)))
    gc = conv_b
    for i in range(CONV_WIDTH):
        gc = gc + gp[:, i:i + S_, :] * conv_w[i]
    return (jax.nn.gelu(gc) * a) @ w_down


def setup_inputs(seed: int = 0) -> dict:
    key = jax.random.key(seed)
    ks = jax.random.split(key, 20)
    D = D_MODEL

    def nrm(k, shape, scale):
        return jax.random.normal(k, shape, jnp.float32) * scale

    return {
        "x": nrm(ks[0], (BATCH, SEQ, D), 1.0),
        "c": nrm(ks[1], (BATCH, D), 1.0),
        "w_ada": nrm(ks[2], (DEPTH, D, 6 * D), 0.2 * D ** -0.5),
        "b_ada": nrm(ks[3], (DEPTH, 6 * D), 0.01),
        "ln_g": 1.0 + nrm(ks[4], (DEPTH, 2, D), 0.02),
        "ln_b": nrm(ks[5], (DEPTH, 2, D), 0.02),
        "w_in_ab": nrm(ks[6], (N_EVEN, D, P_AB), D ** -0.5),
        "w_gate_lr": nrm(ks[7], (N_EVEN, GLA_GATE_RANK, GLA_HEADS * GLA_DK), GLA_GATE_RANK ** -0.5),
        "b_gate": nrm(ks[8], (N_EVEN, GLA_HEADS * GLA_DK), 0.1),
        "gla_norm_g": 1.0 + nrm(ks[9], (N_EVEN, GLA_HEADS * GLA_DV), 0.02),
        "w_pool": nrm(ks[10], (N_EVEN, len(POOL_WINDOWS), POOL_GROUP, POOL_GROUP), POOL_GROUP ** -0.5),
        "pool_scale": 1.0 + nrm(ks[11], (N_EVEN, POOL_WIDTH), 0.02),
        "w_out_ab": nrm(ks[12], (N_EVEN, MIX_AB, D), BETA * MIX_AB ** -0.5),
        "w_qkv": nrm(ks[13], (N_ODD, D, 3 * D), D ** -0.5),
        "rel_bias": nrm(ks[14], (N_ODD, ATT_HEADS, 2 * REL_CLIP + 1), 0.5),
        "w_o": nrm(ks[15], (N_ODD, D, D), BETA * D ** -0.5),
        "w_up": nrm(ks[16], (DEPTH, D, 2 * FFN_DIM), D ** -0.5),
        "conv_w": nrm(ks[17], (DEPTH, CONV_WIDTH, FFN_DIM), CONV_WIDTH ** -0.5),
        "conv_b": nrm(ks[18], (DEPTH, FFN_DIM), 0.02),
        "w_down": nrm(ks[19], (DEPTH, FFN_DIM, D), BETA * FFN_DIM ** -0.5),
    }


def reference(x, c, w_ada, b_ada, ln_g, ln_b, w_in_ab, w_gate_lr, b_gate, gla_norm_g, w_pool,
              pool_scale, w_out_ab, w_qkv, rel_bias, w_o, w_up, conv_w, conv_b, w_down):
    cond = jax.nn.silu(c)
    for l in range(DEPTH):
        mod = (cond @ w_ada[l] + b_ada[l])[:, None, :]
        sh1, sc1, g1, sh2, sc2, g2 = jnp.split(mod, 6, axis=-1)
        u = x * (1.0 + sc1) + sh1
        if l % 2 == 0:
            e = l // 2
            y = even_mixer(u, w_in_ab[e], w_gate_lr[e], b_gate[e], gla_norm_g[e],
                           w_pool[e], pool_scale[e], w_out_ab[e])
        else:
            o = l // 2
            y = odd_mixer(u, w_qkv[o], rel_bias[o], w_o[o])
        x = layer_norm(ALPHA * x + (1.0 + g1) * y, ln_g[l, 0], ln_b[l, 0])
        u = x * (1.0 + sc2) + sh2
        y = conv_ffn(u, w_up[l], conv_w[l], conv_b[l], w_down[l])
        x = layer_norm(ALPHA * x + (1.0 + g2) * y, ln_g[l, 1], ln_b[l, 1])
    return x
```

```python
import functools

import jax
import jax.numpy as jnp
from jax import lax
from jax.experimental import pallas as pl
from jax.experimental.pallas import tpu as pltpu

F32 = jnp.float32
BF16 = jnp.bfloat16

D_MODEL = 2048
CHUNK = 64
POOL_WIDTH = D_MODEL // 2
POOL_WINDOWS = (2, 4, 8, 16)
POOL_GROUP = POOL_WIDTH // len(POOL_WINDOWS)
GLA_HEADS = 4
GLA_DV = (D_MODEL // 2) // GLA_HEADS
GLA_DK = GLA_DV // 2
GLA_GATE_RANK = 16
GLA_TAU = 16.0
OFF_Q = POOL_WIDTH
OFF_K = OFF_Q + GLA_HEADS * GLA_DK
OFF_V = OFF_K + GLA_HEADS * GLA_DK
OFF_R = OFF_V + GLA_HEADS * GLA_DV
OFF_LR = OFF_R + GLA_HEADS * GLA_DV
ATT_HEADS = 16
ATT_HEAD_DIM = D_MODEL // ATT_HEADS
ATT_PAST_CHUNKS = 8
REL_CLIP = 256
FFN_DIM = 5504
LN_EPS = 1e-5
NEG_INF = -1e30

V7X_VMEM_BYTES = 64 * 2**20
LANES = 128
BF16_SUBLANES = 16

FFN_TILE = 512
FFN_PAD = -(-FFN_DIM // FFN_TILE) * FFN_TILE
HALO = BF16_SUBLANES
ATT_Q = 4 * CHUNK
ATT_WIN = ATT_Q + ATT_PAST_CHUNKS * CHUNK
ATT_HB = 4


def _cparams(semantics, vmem_bytes):
    limit = min(int(vmem_bytes), V7X_VMEM_BYTES - 4 * 2**20)
    return pltpu.CompilerParams(dimension_semantics=semantics, vmem_limit_bytes=limit)


def _nbytes(shape, dtype):
    n = 1
    for s in shape:
        n *= s
    return n * jnp.dtype(dtype).itemsize


def _ada_kernel(c_ref, w_ref, b_ref, o_ref):
    c = c_ref[...]
    cond = (c * jax.nn.sigmoid(c)).astype(BF16)
    o_ref[0] = jnp.dot(cond, w_ref[0].astype(BF16), preferred_element_type=F32) + b_ref[0]


def _ada_modulation(c, w_ada, b_ada):
    depth, d, n = w_ada.shape
    bsz = c.shape[0]
    rows = -(-bsz // 8) * 8
    bn = 1024
    c_pad = jnp.pad(c, ((0, rows - bsz), (0, 0)))
    vmem = 2 * (_nbytes((d, bn), F32) + _nbytes((rows, bn), F32)) + _nbytes((d, bn), BF16) + 2**20
    out = pl.pallas_call(
        _ada_kernel,
        grid=(depth, n // bn),
        in_specs=[
            pl.BlockSpec((rows, d), lambda l, j: (0, 0)),
            pl.BlockSpec((1, d, bn), lambda l, j: (l, 0, j)),
            pl.BlockSpec((1, 1, bn), lambda l, j: (l, 0, j)),
        ],
        out_specs=pl.BlockSpec((1, rows, bn), lambda l, j: (l, 0, j)),
        out_shape=jax.ShapeDtypeStruct((depth, rows, n), F32),
        compiler_params=_cparams(("arbitrary", "arbitrary"), vmem),
        name="ada_modulation",
    )(c_pad, w_ada, b_ada.reshape(depth, 1, n))
    return out[:, :bsz]


def _modulate_kernel(x_ref, sc_ref, sh_ref, u_ref):
    u_ref[0] = (x_ref[0] * (1.0 + sc_ref[0]) + sh_ref[0]).astype(BF16)


def _modulate(x, sc, sh):
    bsz, seq, d = x.shape
    bs = min(1024, seq)
    vmem = 2 * (_nbytes((bs, d), F32) + _nbytes((bs, d), BF16)) + 2**20
    return pl.pallas_call(
        _modulate_kernel,
        grid=(bsz, seq // bs),
        in_specs=[
            pl.BlockSpec((1, bs, d), lambda b, s: (b, s, 0)),
            pl.BlockSpec((1, 1, d), lambda b, s: (b, 0, 0)),
            pl.BlockSpec((1, 1, d), lambda b, s: (b, 0, 0)),
        ],
        out_specs=pl.BlockSpec((1, bs, d), lambda b, s: (b, s, 0)),
        out_shape=jax.ShapeDtypeStruct((bsz, seq, d), BF16),
        compiler_params=_cparams(("arbitrary", "arbitrary"), vmem),
        name="modulate",
    )(x, sc, sh)


def _matmul_kernel(a_ref, w_ref, o_ref):
    o_ref[...] = jnp.dot(a_ref[...], w_ref[...], preferred_element_type=F32).astype(o_ref.dtype)


def _matmul(a, w, name):
    m, k = a.shape
    n = w.shape[1]
    bm = min(1024, m)
    bn = min(1024, n)
    vmem = (2 * (_nbytes((bm, k), BF16) + _nbytes((k, bn), BF16) + _nbytes((bm, bn), BF16))
            + _nbytes((bm, bn), F32) + 2**20)
    return pl.pallas_call(
        _matmul_kernel,
        grid=(m // bm, n // bn),
        in_specs=[
            pl.BlockSpec((bm, k), lambda i, j: (i, 0)),
            pl.BlockSpec((k, bn), lambda i, j: (0, j)),
        ],
        out_specs=pl.BlockSpec((bm, bn), lambda i, j: (i, j)),
        out_shape=jax.ShapeDtypeStruct((m, n), BF16),
        compiler_params=_cparams(("arbitrary", "arbitrary"), vmem),
        name=name,
    )(a, w)


def _gate_kernel(u_ref, wlr_ref, wg_ref, bg_ref, o_ref):
    z_lr = jnp.dot(u_ref[...], wlr_ref[...], preferred_element_type=F32).astype(BF16)
    pre = jnp.dot(z_lr, wg_ref[...], preferred_element_type=F32) + bg_ref[...]
    log_sig = jnp.minimum(pre, 0.0) - jnp.log1p(jnp.exp(-jnp.abs(pre)))
    o_ref[...] = log_sig * (1.0 / GLA_TAU)


def _gla_gate(u2d, w_lr, w_gate, b_gate):
    m, k = u2d.shape
    n = w_gate.shape[1]
    bm = min(1024, m)
    vmem = 2 * (_nbytes((bm, k), BF16) + _nbytes((k, LANES), BF16) + _nbytes((bm, n), F32)) + 4 * 2**20
    return pl.pallas_call(
        _gate_kernel,
        grid=(m // bm,),
        in_specs=[
            pl.BlockSpec((bm, k), lambda i: (i, 0)),
            pl.BlockSpec((k, LANES), lambda i: (0, 0)),
            pl.BlockSpec((LANES, n), lambda i: (0, 0)),
            pl.BlockSpec((1, n), lambda i: (0, 0)),
        ],
        out_specs=pl.BlockSpec((bm, n), lambda i: (i, 0)),
        out_shape=jax.ShapeDtypeStruct((m, n), F32),
        compiler_params=_cparams(("arbitrary",), vmem),
        name="gla_gate",
    )(u2d, w_lr, w_gate, b_gate)


def _even_mix_kernel(z_ref, halo_ref, la_ref, wpool_ref, pscale_ref, gng_ref, o_ref, state_ref, *, rows):
    t = pl.program_id(1)
    first = t == 0
    n_chunks = rows // CHUNK

    halo = jnp.where(first, 0.0, halo_ref[0].astype(F32))
    ext = jnp.concatenate([halo, z_ref[0, :, 0:POOL_WIDTH].astype(F32)], axis=0)
    grow = lax.broadcasted_iota(jnp.int32, (rows, 1), 0) + t * rows
    for gi, w in enumerate(POOL_WINDOWS):
        cols = slice(gi * POOL_GROUP, (gi + 1) * POOL_GROUP)
        e = ext[:, cols]
        s = e
        step = 1
        while step < w:
            s = s + pltpu.roll(s, step, 0)
            step *= 2
        count = jnp.minimum(grow + 1, w).astype(F32)
        dev = s[HALO:] / count - e[HALO:]
        y_a = jnp.dot(dev.astype(BF16), wpool_ref[gi], preferred_element_type=F32) * pscale_ref[:, cols]
        o_ref[0, :, cols] = y_a.astype(BF16)

    @pl.when(first)
    def _():
        state_ref[...] = jnp.zeros_like(state_ref)

    in_chunk = lax.broadcasted_iota(jnp.int32, (rows, 1), 0) % CHUNK
    b_all = la_ref[0]
    step = 1
    while step < CHUNK:
        b_all = b_all + jnp.where(in_chunk >= step, pltpu.roll(b_all, step, 0), 0.0)
        step *= 2
    q_all = z_ref[0, :, OFF_Q:OFF_K].astype(F32)
    k_all = z_ref[0, :, OFF_K:OFF_V].astype(F32)
    q_dec = ((q_all * (GLA_DK ** -0.5)) * jnp.exp(b_all)).astype(BF16)
    k_inv = (k_all * jnp.exp(-b_all)).astype(BF16)
    causal = (lax.broadcasted_iota(jnp.int32, (CHUNK, CHUNK), 0)
              >= lax.broadcasted_iota(jnp.int32, (CHUNK, CHUNK), 1))
    states = [state_ref[h] for h in range(GLA_HEADS)]
    for c in range(n_chunks):
        rs = slice(c * CHUNK, (c + 1) * CHUNK)
        b_c = b_all[rs]
        b_last = b_c[CHUNK - 1:CHUNK]
        k_dec = (k_all[rs] * jnp.exp(b_last - b_c)).astype(BF16)
        decay = jnp.exp(b_last)
        v_c = z_ref[0, rs, OFF_V:OFF_R]
        r_c = z_ref[0, rs, OFF_R:OFF_LR].astype(F32)
        for h in range(GLA_HEADS):
            ks = slice(h * GLA_DK, (h + 1) * GLA_DK)
            vs = slice(h * GLA_DV, (h + 1) * GLA_DV)
            qh = q_dec[rs, ks]
            vh = v_c[:, vs]
            scores = lax.dot_general(qh, k_inv[rs, ks], (((1,), (1,)), ((), ())),
                                     preferred_element_type=F32)
            scores = jnp.where(causal, scores, 0.0).astype(BF16)
            o = jnp.dot(scores, vh, preferred_element_type=F32)
            o = o + lax.dot_general(qh, states[h].astype(BF16), (((1,), (1,)), ((), ())),
                                    preferred_element_type=F32)
            update = lax.dot_general(vh, k_dec[:, ks], (((0,), (0,)), ((), ())),
                                     preferred_element_type=F32)
            states[h] = decay[:, ks] * states[h] + update
            mu = jnp.mean(o, axis=-1, keepdims=True)
            oc = o - mu
            var = jnp.mean(oc * oc, axis=-1, keepdims=True)
            o_n = (oc * lax.rsqrt(var + LN_EPS)) * gng_ref[:, vs]
            r_h = r_c[:, vs]
            y_b = (r_h * jax.nn.sigmoid(r_h)) * o_n
            o_ref[0, rs, POOL_WIDTH + h * GLA_DV:POOL_WIDTH + (h + 1) * GLA_DV] = y_b.astype(BF16)
    for h in range(GLA_HEADS):
        state_ref[h] = states[h]


def _even_mix(z, log_a, w_pool, pool_scale, gla_norm_g):
    bsz, seq, zw = z.shape
    rows = min(256, seq)
    hk = GLA_HEADS * GLA_DK
    halo_blocks = rows // HALO
    vmem = (2 * (_nbytes((rows, zw), BF16) + _nbytes((rows, hk), F32) + _nbytes((rows, D_MODEL), BF16)
                 + _nbytes((4, POOL_GROUP, POOL_GROUP), BF16))
            + 12 * _nbytes((rows, POOL_WIDTH), F32) + 4 * 2**20)
    return pl.pallas_call(
        functools.partial(_even_mix_kernel, rows=rows),
        grid=(bsz, seq // rows),
        in_specs=[
            pl.BlockSpec((1, rows, zw), lambda b, t: (b, t, 0)),
            pl.BlockSpec((1, HALO, POOL_WIDTH), lambda b, t: (b, jnp.maximum(t * halo_blocks - 1, 0), 0)),
            pl.BlockSpec((1, rows, hk), lambda b, t: (b, t, 0)),
            pl.BlockSpec((len(POOL_WINDOWS), POOL_GROUP, POOL_GROUP), lambda b, t: (0, 0, 0)),
            pl.BlockSpec((1, POOL_WIDTH), lambda b, t: (0, 0)),
            pl.BlockSpec((1, GLA_HEADS * GLA_DV), lambda b, t: (0, 0)),
        ],
        out_specs=pl.BlockSpec((1, rows, D_MODEL), lambda b, t: (b, t, 0)),
        out_shape=jax.ShapeDtypeStruct((bsz, seq, D_MODEL), BF16),
        scratch_shapes=[pltpu.VMEM((GLA_HEADS, GLA_DV, GLA_DK), F32)],
        compiler_params=_cparams(("arbitrary", "arbitrary"), vmem),
        name="even_mix",
    )(z, z, log_a, w_pool, pool_scale, gla_norm_g)


def _attn_kernel(q_ref, k_ref, v_ref, bias_ref, o_ref, kwin_ref, vwin_ref):
    i = pl.program_id(2)
    past = ATT_WIN - ATT_Q

    @pl.when(i == 0)
    def _():
        kwin_ref[0:past] = jnp.zeros((past, kwin_ref.shape[1]), BF16)
        vwin_ref[0:past] = jnp.zeros((past, vwin_ref.shape[1]), BF16)

    @pl.when(i > 0)
    def _():
        for lo in range(0, past, ATT_Q):
            kwin_ref[lo:lo + ATT_Q] = kwin_ref[lo + ATT_Q:lo + 2 * ATT_Q]
            vwin_ref[lo:lo + ATT_Q] = vwin_ref[lo + ATT_Q:lo + 2 * ATT_Q]

    kwin_ref[past:ATT_WIN] = k_ref[0]
    vwin_ref[past:ATT_WIN] = v_ref[0]

    kpos = lax.broadcasted_iota(jnp.int32, (ATT_Q, ATT_WIN), 1)
    valid = kpos >= past - i * ATT_Q
    scale = ATT_HEAD_DIM ** -0.5
    for h in range(bias_ref.shape[0]):
        cs = slice(h * ATT_HEAD_DIM, (h + 1) * ATT_HEAD_DIM)
        s = lax.dot_general(q_ref[0, :, cs], kwin_ref[:, cs], (((1,), (1,)), ((), ())),
                            preferred_element_type=F32)
        s = s * scale + bias_ref[h]
        s = jnp.where(valid, s, NEG_INF)
        m = jnp.max(s, axis=-1, keepdims=True)
        p = jnp.exp(s - m)
        denom = jnp.sum(p, axis=-1, keepdims=True)
        o = jnp.dot(p.astype(BF16), vwin_ref[:, cs], preferred_element_type=F32)
        o_ref[0, :, cs] = (o / denom).astype(BF16)


def _band_attention(qkv, bias_tab):
    bsz, seq, _ = qkv.shape
    hb = ATT_HB
    width = hb * ATT_HEAD_DIM
    groups = ATT_HEADS // hb
    vmem = (2 * (4 * _nbytes((ATT_Q, width), BF16) + _nbytes((hb, ATT_Q, ATT_WIN), F32))
            + 2 * _nbytes((ATT_WIN, width), BF16) + 8 * _nbytes((ATT_Q, ATT_WIN), F32) + 4 * 2**20)
    return pl.pallas_call(
        _attn_kernel,
        grid=(groups, bsz, seq // ATT_Q),
        in_specs=[
            pl.BlockSpec((1, ATT_Q, width), lambda g, b, i: (b, i, g)),
            pl.BlockSpec((1, ATT_Q, width), lambda g, b, i: (b, i, groups + g)),
            pl.BlockSpec((1, ATT_Q, width), lambda g, b, i: (b, i, 2 * groups + g)),
            pl.BlockSpec((hb, ATT_Q, ATT_WIN), lambda g, b, i: (g, 0, 0)),
        ],
        out_specs=pl.BlockSpec((1, ATT_Q, width), lambda g, b, i: (b, i, g)),
        out_shape=jax.ShapeDtypeStruct((bsz, seq, D_MODEL), BF16),
        scratch_shapes=[pltpu.VMEM((ATT_WIN, width), BF16), pltpu.VMEM((ATT_WIN, width), BF16)],
        compiler_params=_cparams(("arbitrary", "arbitrary", "arbitrary"), vmem),
        name="band_attention",
    )(qkv, qkv, qkv, bias_tab)


def _attention_bias_table(rel_bias):
    qpos = jnp.arange(ATT_Q)[:, None]
    kpos = jnp.arange(ATT_WIN)[None, :]
    pad = ATT_PAST_CHUNKS * CHUNK
    rel = jnp.clip(pad + qpos - kpos, -REL_CLIP, REL_CLIP) + REL_CLIP
    lag = kpos // CHUNK - qpos // CHUNK
    in_band = (lag >= 0) & (lag <= ATT_PAST_CHUNKS)
    return jnp.where(in_band[None], rel_bias[:, rel].astype(F32), NEG_INF)


def _mm_ln_kernel(a_ref, w_ref, x_ref, gate_ref, lng_ref, lnb_ref, *rest, nk, alpha, emit_u):
    if emit_u:
        sc_ref, sh_ref, xo_ref, uo_ref = rest[:4]
        rest = rest[4:]
    else:
        xo_ref = rest[0]
        rest = rest[1:]
    part = jnp.dot(a_ref[0], w_ref[...], preferred_element_type=F32)

    def epilogue(y):
        t = alpha * x_ref[0] + (1.0 + gate_ref[0]) * y
        mu = jnp.mean(t, axis=-1, keepdims=True)
        tc = t - mu
        var = jnp.mean(tc * tc, axis=-1, keepdims=True)
        xn = (tc * lax.rsqrt(var + LN_EPS)) * lng_ref[...] + lnb_ref[...]
        xo_ref[0] = xn
        if emit_u:
            uo_ref[0] = (xn * (1.0 + sc_ref[0]) + sh_ref[0]).astype(BF16)

    if nk == 1:
        epilogue(part)
    else:
        acc_ref = rest[0]
        k = pl.program_id(2)

        @pl.when(k == 0)
        def _():
            acc_ref[...] = part

        @pl.when(k > 0)
        def _():
            acc_ref[...] += part

        @pl.when(k == nk - 1)
        def _():
            epilogue(acc_ref[...])


def _mm_ln(a, w, x, gate, ln_g, ln_b, next_sc, next_sh, alpha, bk, name):
    bsz, seq, kdim = a.shape
    d = w.shape[1]
    bm = min(512, seq)
    nk = kdim // bk
    emit_u = next_sc is not None
    vec = pl.BlockSpec((1, 1, d), lambda b, i, k: (b, 0, 0))
    par = pl.BlockSpec((1, d), lambda b, i, k: (0, 0))
    row = pl.BlockSpec((1, bm, d), lambda b, i, k: (b, i, 0))
    in_specs = [
        pl.BlockSpec((1, bm, bk), lambda b, i, k: (b, i, k)),
        pl.BlockSpec((bk, d), lambda b, i, k: (k, 0)),
        row, vec, par, par,
    ]
    args = [a, w, x, gate, ln_g, ln_b]
    out_specs = [row]
    out_shape = [jax.ShapeDtypeStruct((bsz, seq, d), F32)]
    if emit_u:
        in_specs += [vec, vec]
        args += [next_sc, next_sh]
        out_specs.append(row)
        out_shape.append(jax.ShapeDtypeStruct((bsz, seq, d), BF16))
    scratch = [pltpu.VMEM((bm, d), F32)] if nk > 1 else []
    vmem = (2 * (_nbytes((bm, bk), BF16) + _nbytes((bk, d), BF16) + 2 * _nbytes((bm, d), F32)
                 + _nbytes((bm, d), BF16))
            + 3 * _nbytes((bm, d), F32) + 2 * 2**20)
    outs = pl.pallas_call(
        functools.partial(_mm_ln_kernel, nk=nk, alpha=alpha, emit_u=emit_u),
        grid=(bsz, seq // bm, nk),
        in_specs=in_specs,
        out_specs=out_specs,
        out_shape=out_shape,
        scratch_shapes=scratch,
        compiler_params=_cparams(("arbitrary", "arbitrary", "arbitrary"), vmem),
        name=name,
    )(*args)
    return (outs[0], outs[1]) if emit_u else (outs[0], None)


def _ffn_up_kernel(u_ref, halo_ref, wa_ref, wg_ref, cw_ref, cb_ref, h_ref):
    m = pl.program_id(1)
    u = u_ref[0]
    halo = halo_ref[0]
    halo = jnp.where(m == 0, jnp.zeros_like(halo), halo)
    g = jnp.dot(jnp.concatenate([halo, u], axis=0), wg_ref[...], preferred_element_type=F32)
    a = jnp.dot(u, wa_ref[...], preferred_element_type=F32)
    gc = (cb_ref[...]
          + pltpu.roll(g, 2, 0)[HALO:] * cw_ref[0:1]
          + pltpu.roll(g, 1, 0)[HALO:] * cw_ref[1:2]
          + g[HALO:] * cw_ref[2:3])
    h_ref[0] = (jax.nn.gelu(gc) * a).astype(BF16)


def _ffn_up(u, w_up, conv_w, conv_b):
    bsz, seq, d = u.shape
    bm = min(1024, seq)
    bn = FFN_TILE
    n_tiles = FFN_PAD // bn
    halo_blocks = bm // HALO
    vmem = (2 * (_nbytes((bm, d), BF16) + 2 * _nbytes((d, bn), BF16) + _nbytes((bm, bn), BF16))
            + 8 * _nbytes((bm + HALO, bn), F32) + _nbytes((bm + HALO, d), BF16) + 4 * 2**20)
    return pl.pallas_call(
        _ffn_up_kernel,
        grid=(bsz, seq // bm, n_tiles),
        in_specs=[
            pl.BlockSpec((1, bm, d), lambda b, i, j: (b, i, 0)),
            pl.BlockSpec((1, HALO, d), lambda b, i, j: (b, jnp.maximum(i * halo_blocks - 1, 0), 0)),
            pl.BlockSpec((d, bn), lambda b, i, j: (0, j)),
            pl.BlockSpec((d, bn), lambda b, i, j: (0, n_tiles + j)),
            pl.BlockSpec((3, bn), lambda b, i, j: (0, j)),
            pl.BlockSpec((1, bn), lambda b, i, j: (0, j)),
        ],
        out_specs=pl.BlockSpec((1, bm, bn), lambda b, i, j: (b, i, j)),
        out_shape=jax.ShapeDtypeStruct((bsz, seq, FFN_PAD), BF16),
        compiler_params=_cparams(("arbitrary", "arbitrary", "arbitrary"), vmem),
        name="ffn_up",
    )(u, u, w_up, w_up, conv_w, conv_b)


def kernel(x, c, w_ada, b_ada, ln_g, ln_b, w_in_ab, w_gate_lr, b_gate, gla_norm_g, w_pool, pool_scale,
           w_out_ab, w_qkv, rel_bias, w_o, w_up, conv_w, conv_b, w_down):
    bsz, seq, d = x.shape
    depth = w_ada.shape[0]
    alpha = (2.0 * depth) ** 0.25
    fpad = FFN_PAD - FFN_DIM

    mod = _ada_modulation(c, w_ada, b_ada).reshape(depth, bsz, 6, 1, d)

    def mod_vec(layer, idx):
        return mod[layer, :, idx]

    u = _modulate(x, mod_vec(0, 1), mod_vec(0, 0))
    for layer in range(depth):
        if layer % 2 == 0:
            e = layer // 2
            w_main = w_in_ab[e, :, :OFF_LR].astype(BF16)
            w_lr = jnp.pad(w_in_ab[e, :, OFF_LR:], ((0, 0), (0, LANES - GLA_GATE_RANK))).astype(BF16)
            w_gate = jnp.pad(w_gate_lr[e], ((0, LANES - GLA_GATE_RANK), (0, 0))).astype(BF16)
            u2d = u.reshape(bsz * seq, d)
            z = _matmul(u2d, w_main, "in_proj").reshape(bsz, seq, OFF_LR)
            log_a = _gla_gate(u2d, w_lr, w_gate, b_gate[e][None]).reshape(bsz, seq, GLA_HEADS * GLA_DK)
            y = _even_mix(z, log_a, w_pool[e].astype(BF16), pool_scale[e][None], gla_norm_g[e][None])
            w_proj = w_out_ab[e].astype(BF16)
        else:
            o = layer // 2
            qkv = _matmul(u.reshape(bsz * seq, d), w_qkv[o].astype(BF16), "qkv_proj").reshape(bsz, seq, 3 * d)
            y = _band_attention(qkv, _attention_bias_table(rel_bias[o]))
            w_proj = w_o[o].astype(BF16)
        x, u = _mm_ln(y, w_proj, x, mod_vec(layer, 2), ln_g[layer, 0][None], ln_b[layer, 0][None],
                      mod_vec(layer, 4), mod_vec(layer, 3), alpha, d, "mixer_out_ln")

        w_up_pad = jnp.concatenate([
            jnp.pad(w_up[layer, :, :FFN_DIM], ((0, 0), (0, fpad))),
            jnp.pad(w_up[layer, :, FFN_DIM:], ((0, 0), (0, fpad)))], axis=1).astype(BF16)
        h = _ffn_up(u, w_up_pad, jnp.pad(conv_w[layer], ((0, 0), (0, fpad))),
                    jnp.pad(conv_b[layer], (0, fpad))[None])
        w_down_pad = jnp.pad(w_down[layer], ((0, fpad), (0, 0))).astype(BF16)
        last = layer == depth - 1
        x, u = _mm_ln(h, w_down_pad, x, mod_vec(layer, 5), ln_g[layer, 1][None], ln_b[layer, 1][None],
                      None if last else mod_vec(layer + 1, 1), None if last else mod_vec(layer + 1, 0),
                      alpha, FFN_PAD // 4, "ffn_down_ln")
    return x
```

```python
import functools

import jax
import jax.numpy as jnp
from jax import lax
from jax.experimental import pallas as pl
from jax.experimental.pallas import tpu as pltpu

F32 = jnp.float32
BF16 = jnp.bfloat16

D_MODEL = 2048
CHUNK = 64
POOL_WIDTH = D_MODEL // 2
POOL_WINDOWS = (2, 4, 8, 16)
POOL_GROUP = POOL_WIDTH // len(POOL_WINDOWS)
GLA_HEADS = 4
GLA_DV = (D_MODEL // 2) // GLA_HEADS
GLA_DK = GLA_DV // 2
GLA_GATE_RANK = 16
GLA_TAU = 16.0
OFF_Q = POOL_WIDTH
OFF_K = OFF_Q + GLA_HEADS * GLA_DK
OFF_V = OFF_K + GLA_HEADS * GLA_DK
OFF_R = OFF_V + GLA_HEADS * GLA_DV
OFF_LR = OFF_R + GLA_HEADS * GLA_DV
ATT_HEADS = 16
ATT_HEAD_DIM = D_MODEL // ATT_HEADS
ATT_PAST_CHUNKS = 8
REL_CLIP = 256
FFN_DIM = 5504
LN_EPS = 1e-5
NEG_INF = -1e30

V7X_VMEM_BYTES = 64 * 2**20
LANES = 128
BF16_SUBLANES = 16

FFN_TILE = 512
FFN_PAD = -(-FFN_DIM // FFN_TILE) * FFN_TILE
HALO = BF16_SUBLANES
ATT_Q = 4 * CHUNK
ATT_WIN = ATT_Q + ATT_PAST_CHUNKS * CHUNK
ATT_HB = 4


def _cparams(semantics, vmem_bytes):
    limit = min(int(vmem_bytes), V7X_VMEM_BYTES - 4 * 2**20)
    return pltpu.CompilerParams(dimension_semantics=semantics, vmem_limit_bytes=limit)


def _nbytes(shape, dtype):
    n = 1
    for s in shape:
        n *= s
    return n * jnp.dtype(dtype).itemsize


def _ada_kernel(c_ref, w_ref, b_ref, o_ref):
    c = c_ref[...]
    cond = (c * jax.nn.sigmoid(c)).astype(BF16)
    o_ref[0] = jnp.dot(cond, w_ref[0].astype(BF16), preferred_element_type=F32) + b_ref[0]


def _ada_modulation(c, w_ada, b_ada):
    depth, d, n = w_ada.shape
    bsz = c.shape[0]
    rows = -(-bsz // 8) * 8
    bn = 1024
    c_pad = jnp.pad(c, ((0, rows - bsz), (0, 0)))
    vmem = 2 * (_nbytes((d, bn), F32) + _nbytes((rows, bn), F32)) + _nbytes((d, bn), BF16) + 2**20
    out = pl.pallas_call(
        _ada_kernel,
        grid=(depth, n // bn),
        in_specs=[
            pl.BlockSpec((rows, d), lambda l, j: (0, 0)),
            pl.BlockSpec((1, d, bn), lambda l, j: (l, 0, j)),
            pl.BlockSpec((1, 1, bn), lambda l, j: (l, 0, j)),
        ],
        out_specs=pl.BlockSpec((1, rows, bn), lambda l, j: (l, 0, j)),
        out_shape=jax.ShapeDtypeStruct((depth, rows, n), F32),
        compiler_params=_cparams(("arbitrary", "arbitrary"), vmem),
        name="ada_modulation",
    )(c_pad, w_ada, b_ada.reshape(depth, 1, n))
    return out[:, :bsz]


def _modulate_kernel(x_ref, sc_ref, sh_ref, u_ref):
    u_ref[0] = (x_ref[0] * (1.0 + sc_ref[0]) + sh_ref[0]).astype(BF16)


def _modulate(x, sc, sh):
    bsz, seq, d = x.shape
    bs = min(1024, seq)
    vmem = 2 * (_nbytes((bs, d), F32) + _nbytes((bs, d), BF16)) + 2**20
    return pl.pallas_call(
        _modulate_kernel,
        grid=(bsz, seq // bs),
        in_specs=[
            pl.BlockSpec((1, bs, d), lambda b, s: (b, s, 0)),
            pl.BlockSpec((1, 1, d), lambda b, s: (b, 0, 0)),
            pl.BlockSpec((1, 1, d), lambda b, s: (b, 0, 0)),
        ],
        out_specs=pl.BlockSpec((1, bs, d), lambda b, s: (b, s, 0)),
        out_shape=jax.ShapeDtypeStruct((bsz, seq, d), BF16),
        compiler_params=_cparams(("arbitrary", "arbitrary"), vmem),
        name="modulate",
    )(x, sc, sh)


def _matmul_kernel(a_ref, w_ref, o_ref):
    o_ref[...] = jnp.dot(a_ref[...], w_ref[...], preferred_element_type=F32).astype(o_ref.dtype)


def _matmul(a, w, name):
    m, k = a.shape
    n = w.shape[1]
    bm = min(1024, m)
    bn = min(1024, n)
    vmem = (2 * (_nbytes((bm, k), BF16) + _nbytes((k, bn), BF16) + _nbytes((bm, bn), BF16))
            + _nbytes((bm, bn), F32) + 2**20)
    return pl.pallas_call(
        _matmul_kernel,
        grid=(m // bm, n // bn),
        in_specs=[
            pl.BlockSpec((bm, k), lambda i, j: (i, 0)),
            pl.BlockSpec((k, bn), lambda i, j: (0, j)),
        ],
        out_specs=pl.BlockSpec((bm, bn), lambda i, j: (i, j)),
        out_shape=jax.ShapeDtypeStruct((m, n), BF16),
        compiler_params=_cparams(("arbitrary", "arbitrary"), vmem),
        name=name,
    )(a, w)


def _gate_kernel(u_ref, wlr_ref, wg_ref, bg_ref, o_ref):
    z_lr = jnp.dot(u_ref[...], wlr_ref[...], preferred_element_type=F32).astype(BF16)
    pre = jnp.dot(z_lr, wg_ref[...], preferred_element_type=F32) + bg_ref[...]
    log_sig = jnp.minimum(pre, 0.0) - jnp.log1p(jnp.exp(-jnp.abs(pre)))
    o_ref[...] = log_sig * (1.0 / GLA_TAU)


def _gla_gate(u2d, w_lr, w_gate, b_gate):
    m, k = u2d.shape
    n = w_gate.shape[1]
    bm = min(1024, m)
    vmem = 2 * (_nbytes((bm, k), BF16) + _nbytes((k, LANES), BF16) + _nbytes((bm, n), F32)) + 4 * 2**20
    return pl.pallas_call(
        _gate_kernel,
        grid=(m // bm,),
        in_specs=[
            pl.BlockSpec((bm, k), lambda i: (i, 0)),
            pl.BlockSpec((k, LANES), lambda i: (0, 0)),
            pl.BlockSpec((LANES, n), lambda i: (0, 0)),
            pl.BlockSpec((1, n), lambda i: (0, 0)),
        ],
        out_specs=pl.BlockSpec((bm, n), lambda i: (i, 0)),
        out_shape=jax.ShapeDtypeStruct((m, n), F32),
        compiler_params=_cparams(("arbitrary",), vmem),
        name="gla_gate",
    )(u2d, w_lr, w_gate, b_gate)


def _even_mix_kernel(z_ref, halo_ref, la_ref, wpool_ref, pscale_ref, gng_ref, o_ref, state_ref, *, rows):
    t = pl.program_id(1)
    first = t == 0
    n_chunks = rows // CHUNK

    halo = jnp.where(first, 0.0, halo_ref[0].astype(F32))
    ext = jnp.concatenate([halo, z_ref[0, :, 0:POOL_WIDTH].astype(F32)], axis=0)
    grow = lax.broadcasted_iota(jnp.int32, (rows, 1), 0) + t * rows
    for gi, w in enumerate(POOL_WINDOWS):
        cols = slice(gi * POOL_GROUP, (gi + 1) * POOL_GROUP)
        e = ext[:, cols]
        s = e
        step = 1
        while step < w:
            s = s + pltpu.roll(s, step, 0)
            step *= 2
        count = jnp.minimum(grow + 1, w).astype(F32)
        dev = s[HALO:] / count - e[HALO:]
        y_a = jnp.dot(dev.astype(BF16), wpool_ref[gi], preferred_element_type=F32) * pscale_ref[:, cols]
        o_ref[0, :, cols] = y_a.astype(BF16)

    @pl.when(first)
    def _():
        state_ref[...] = jnp.zeros_like(state_ref)

    in_chunk = lax.broadcasted_iota(jnp.int32, (rows, 1), 0) % CHUNK
    b_all = la_ref[0]
    step = 1
    while step < CHUNK:
        b_all = b_all + jnp.where(in_chunk >= step, pltpu.roll(b_all, step, 0), 0.0)
        step *= 2
    q_all = z_ref[0, :, OFF_Q:OFF_K].astype(F32)
    k_all = z_ref[0, :, OFF_K:OFF_V].astype(F32)
    q_dec = ((q_all * (GLA_DK ** -0.5)) * jnp.exp(b_all)).astype(BF16)
    k_inv = (k_all * jnp.exp(-b_all)).astype(BF16)
    causal = (lax.broadcasted_iota(jnp.int32, (CHUNK, CHUNK), 0)
              >= lax.broadcasted_iota(jnp.int32, (CHUNK, CHUNK), 1))
    states = [state_ref[h] for h in range(GLA_HEADS)]
    for c in range(n_chunks):
        rs = slice(c * CHUNK, (c + 1) * CHUNK)
        b_c = b_all[rs]
        b_last = b_c[CHUNK - 1:CHUNK]
        k_dec = (k_all[rs] * jnp.exp(b_last - b_c)).astype(BF16)
        decay = jnp.exp(b_last)
        v_c = z_ref[0, rs, OFF_V:OFF_R]
        r_c = z_ref[0, rs, OFF_R:OFF_LR].astype(F32)
        for h in range(GLA_HEADS):
            ks = slice(h * GLA_DK, (h + 1) * GLA_DK)
            vs = slice(h * GLA_DV, (h + 1) * GLA_DV)
            qh = q_dec[rs, ks]
            vh = v_c[:, vs]
            scores = lax.dot_general(qh, k_inv[rs, ks], (((1,), (1,)), ((), ())),
                                     preferred_element_type=F32)
            scores = jnp.where(causal, scores, 0.0).astype(BF16)
            o = jnp.dot(scores, vh, preferred_element_type=F32)
            o = o + lax.dot_general(qh, states[h].astype(BF16), (((1,), (1,)), ((), ())),
                                    preferred_element_type=F32)
            update = lax.dot_general(vh, k_dec[:, ks], (((0,), (0,)), ((), ())),
                                     preferred_element_type=F32)
            states[h] = decay[:, ks] * states[h] + update
            mu = jnp.mean(o, axis=-1, keepdims=True)
            oc = o - mu
            var = jnp.mean(oc * oc, axis=-1, keepdims=True)
            o_n = (oc * lax.rsqrt(var + LN_EPS)) * gng_ref[:, vs]
            r_h = r_c[:, vs]
            y_b = (r_h * jax.nn.sigmoid(r_h)) * o_n
            o_ref[0, rs, POOL_WIDTH + h * GLA_DV:POOL_WIDTH + (h + 1) * GLA_DV] = y_b.astype(BF16)
    for h in range(GLA_HEADS):
        state_ref[h] = states[h]


def _even_mix(z, log_a, w_pool, pool_scale, gla_norm_g):
    bsz, seq, zw = z.shape
    rows = min(256, seq)
    hk = GLA_HEADS * GLA_DK
    halo_blocks = rows // HALO
    vmem = (2 * (_nbytes((rows, zw), BF16) + _nbytes((rows, hk), F32) + _nbytes((rows, D_MODEL), BF16)
                 + _nbytes((4, POOL_GROUP, POOL_GROUP), BF16))
            + 12 * _nbytes((rows, POOL_WIDTH), F32) + 4 * 2**20)
    return pl.pallas_call(
        functools.partial(_even_mix_kernel, rows=rows),
        grid=(bsz, seq // rows),
        in_specs=[
            pl.BlockSpec((1, rows, zw), lambda b, t: (b, t, 0)),
            pl.BlockSpec((1, HALO, POOL_WIDTH), lambda b, t: (b, jnp.maximum(t * halo_blocks - 1, 0), 0)),
            pl.BlockSpec((1, rows, hk), lambda b, t: (b, t, 0)),
            pl.BlockSpec((len(POOL_WINDOWS), POOL_GROUP, POOL_GROUP), lambda b, t: (0, 0, 0)),
            pl.BlockSpec((1, POOL_WIDTH), lambda b, t: (0, 0)),
            pl.BlockSpec((1, GLA_HEADS * GLA_DV), lambda b, t: (0, 0)),
        ],
        out_specs=pl.BlockSpec((1, rows, D_MODEL), lambda b, t: (b, t, 0)),
        out_shape=jax.ShapeDtypeStruct((bsz, seq, D_MODEL), BF16),
        scratch_shapes=[pltpu.VMEM((GLA_HEADS, GLA_DV, GLA_DK), F32)],
        compiler_params=_cparams(("arbitrary", "arbitrary"), vmem),
        name="even_mix",
    )(z, z, log_a, w_pool, pool_scale, gla_norm_g)


def _attn_kernel(q_ref, k_ref, v_ref, bias_ref, o_ref, kwin_ref, vwin_ref):
    i = pl.program_id(2)
    past = ATT_WIN - ATT_Q

    @pl.when(i == 0)
    def _():
        kwin_ref[0:past] = jnp.zeros((past, kwin_ref.shape[1]), BF16)
        vwin_ref[0:past] = jnp.zeros((past, vwin_ref.shape[1]), BF16)

    @pl.when(i > 0)
    def _():
        for lo in range(0, past, ATT_Q):
            kwin_ref[lo:lo + ATT_Q] = kwin_ref[lo + ATT_Q:lo + 2 * ATT_Q]
            vwin_ref[lo:lo + ATT_Q] = vwin_ref[lo + ATT_Q:lo + 2 * ATT_Q]

    kwin_ref[past:ATT_WIN] = k_ref[0]
    vwin_ref[past:ATT_WIN] = v_ref[0]

    kpos = lax.broadcasted_iota(jnp.int32, (ATT_Q, ATT_WIN), 1)
    valid = kpos >= past - i * ATT_Q
    scale = ATT_HEAD_DIM ** -0.5
    for h in range(bias_ref.shape[0]):
        cs = slice(h * ATT_HEAD_DIM, (h + 1) * ATT_HEAD_DIM)
        s = lax.dot_general(q_ref[0, :, cs], kwin_ref[:, cs], (((1,), (1,)), ((), ())),
                            preferred_element_type=F32)
        s = jnp.where(valid, s * scale + bias_ref[h], NEG_INF)
        m = jnp.max(s, axis=-1, keepdims=True)
        p = jnp.exp(s - m)
        denom = jnp.sum(p, axis=-1, keepdims=True)
        o = jnp.dot(p.astype(BF16), vwin_ref[:, cs], preferred_element_type=F32)
        o_ref[0, :, cs] = (o / denom).astype(BF16)


def _band_attention(qkv, bias_tab):
    bsz, seq, _ = qkv.shape
    hb = ATT_HB
    width = hb * ATT_HEAD_DIM
    groups = ATT_HEADS // hb
    vmem = (2 * (4 * _nbytes((ATT_Q, width), BF16) + _nbytes((hb, ATT_Q, ATT_WIN), F32))
            + 2 * _nbytes((ATT_WIN, width), BF16) + 8 * _nbytes((ATT_Q, ATT_WIN), F32) + 4 * 2**20)
    return pl.pallas_call(
        _attn_kernel,
        grid=(groups, bsz, seq // ATT_Q),
        in_specs=[
            pl.BlockSpec((1, ATT_Q, width), lambda g, b, i: (b, i, g)),
            pl.BlockSpec((1, ATT_Q, width), lambda g, b, i: (b, i, groups + g)),
            pl.BlockSpec((1, ATT_Q, width), lambda g, b, i: (b, i, 2 * groups + g)),
            pl.BlockSpec((hb, ATT_Q, ATT_WIN), lambda g, b, i: (g, 0, 0)),
        ],
        out_specs=pl.BlockSpec((1, ATT_Q, width), lambda g, b, i: (b, i, g)),
        out_shape=jax.ShapeDtypeStruct((bsz, seq, D_MODEL), BF16),
        scratch_shapes=[pltpu.VMEM((ATT_WIN, width), BF16), pltpu.VMEM((ATT_WIN, width), BF16)],
        compiler_params=_cparams(("arbitrary", "arbitrary", "arbitrary"), vmem),
        name="band_attention",
    )(qkv, qkv, qkv, bias_tab)


def _attention_bias_table(rel_bias):
    pad = ATT_PAST_CHUNKS * CHUNK
    offs = (ATT_Q - 1) - jnp.arange(ATT_WIN + ATT_Q - 1)
    diag = rel_bias[:, jnp.clip(pad + offs, -REL_CLIP, REL_CLIP) + REL_CLIP].astype(F32)
    table = jnp.stack([diag[:, ATT_Q - 1 - q:ATT_Q - 1 - q + ATT_WIN] for q in range(ATT_Q)], axis=1)
    lag = jnp.arange(ATT_WIN)[None, :] // CHUNK - jnp.arange(ATT_Q)[:, None] // CHUNK
    in_band = (lag >= 0) & (lag <= ATT_PAST_CHUNKS)
    return jnp.where(in_band[None], table, NEG_INF)


def _mm_ln_kernel(a_ref, w_ref, x_ref, gate_ref, lng_ref, lnb_ref, *rest, alpha, emit_u, sub):
    if emit_u:
        sc_ref, sh_ref, xo_ref, uo_ref = rest
    else:
        (xo_ref,) = rest
    for r in range(a_ref.shape[1] // sub):
        rows = slice(r * sub, (r + 1) * sub)
        y = jnp.dot(a_ref[0, rows, :], w_ref[...], preferred_element_type=F32)
        t = alpha * x_ref[0, rows, :] + (1.0 + gate_ref[0]) * y
        mu = jnp.mean(t, axis=-1, keepdims=True)
        tc = t - mu
        var = jnp.mean(tc * tc, axis=-1, keepdims=True)
        xn = (tc * lax.rsqrt(var + LN_EPS)) * lng_ref[...] + lnb_ref[...]
        xo_ref[0, rows, :] = xn
        if emit_u:
            uo_ref[0, rows, :] = (xn * (1.0 + sc_ref[0]) + sh_ref[0]).astype(BF16)


def _mm_ln(a, w, x, gate, ln_g, ln_b, next_sc, next_sh, alpha, bm, name):
    bsz, seq, kdim = a.shape
    d = w.shape[1]
    bm = min(bm, seq)
    sub = min(256, bm)
    emit_u = next_sc is not None
    vec = pl.BlockSpec((1, 1, d), lambda b, i: (b, 0, 0))
    par = pl.BlockSpec((1, d), lambda b, i: (0, 0))
    row = pl.BlockSpec((1, bm, d), lambda b, i: (b, i, 0))
    in_specs = [
        pl.BlockSpec((1, bm, kdim), lambda b, i: (b, i, 0)),
        pl.BlockSpec((kdim, d), lambda b, i: (0, 0), pipeline_mode=pl.Buffered(1)),
        row, vec, par, par,
    ]
    args = [a, w, x, gate, ln_g, ln_b]
    out_specs = [row]
    out_shape = [jax.ShapeDtypeStruct((bsz, seq, d), F32)]
    if emit_u:
        in_specs += [vec, vec]
        args += [next_sc, next_sh]
        out_specs.append(row)
        out_shape.append(jax.ShapeDtypeStruct((bsz, seq, d), BF16))
    vmem = (_nbytes((kdim, d), BF16)
            + 2 * (_nbytes((bm, kdim), BF16) + 2 * _nbytes((bm, d), F32) + _nbytes((bm, d), BF16))
            + 6 * _nbytes((sub, d), F32) + 2 * 2**20)
    outs = pl.pallas_call(
        functools.partial(_mm_ln_kernel, alpha=alpha, emit_u=emit_u, sub=sub),
        grid=(bsz, seq // bm),
        in_specs=in_specs,
        out_specs=out_specs,
        out_shape=out_shape,
        compiler_params=_cparams(("arbitrary", "arbitrary"), vmem),
        name=name,
    )(*args)
    return (outs[0], outs[1]) if emit_u else (outs[0], None)


def _ffn_up_kernel(u_ref, halo_ref, wa_ref, wg_ref, cw_ref, cb_ref, h_ref):
    m = pl.program_id(1)
    u = u_ref[0]
    halo = halo_ref[0]
    halo = jnp.where(m == 0, jnp.zeros_like(halo), halo)
    g = jnp.dot(jnp.concatenate([halo, u], axis=0), wg_ref[...], preferred_element_type=F32)
    a = jnp.dot(u, wa_ref[...], preferred_element_type=F32)
    gc = (cb_ref[...]
          + pltpu.roll(g, 2, 0)[HALO:] * cw_ref[0:1]
          + pltpu.roll(g, 1, 0)[HALO:] * cw_ref[1:2]
          + g[HALO:] * cw_ref[2:3])
    h_ref[0] = (jax.nn.gelu(gc) * a).astype(BF16)


def _ffn_up(u, w_up, conv_w, conv_b):
    bsz, seq, d = u.shape
    bm = min(1024, seq)
    bn = FFN_TILE
    n_tiles = FFN_PAD // bn
    halo_blocks = bm // HALO
    vmem = (2 * (_nbytes((bm, d), BF16) + 2 * _nbytes((d, bn), BF16) + _nbytes((bm, bn), BF16))
            + 8 * _nbytes((bm + HALO, bn), F32) + _nbytes((bm + HALO, d), BF16) + 4 * 2**20)
    return pl.pallas_call(
        _ffn_up_kernel,
        grid=(bsz, seq // bm, n_tiles),
        in_specs=[
            pl.BlockSpec((1, bm, d), lambda b, i, j: (b, i, 0)),
            pl.BlockSpec((1, HALO, d), lambda b, i, j: (b, jnp.maximum(i * halo_blocks - 1, 0), 0)),
            pl.BlockSpec((d, bn), lambda b, i, j: (0, j)),
            pl.BlockSpec((d, bn), lambda b, i, j: (0, n_tiles + j)),
            pl.BlockSpec((3, bn), lambda b, i, j: (0, j)),
            pl.BlockSpec((1, bn), lambda b, i, j: (0, j)),
        ],
        out_specs=pl.BlockSpec((1, bm, bn), lambda b, i, j: (b, i, j)),
        out_shape=jax.ShapeDtypeStruct((bsz, seq, FFN_PAD), BF16),
        compiler_params=_cparams(("arbitrary", "arbitrary", "arbitrary"), vmem),
        name="ffn_up",
    )(u, u, w_up, w_up, conv_w, conv_b)


def kernel(x, c, w_ada, b_ada, ln_g, ln_b, w_in_ab, w_gate_lr, b_gate, gla_norm_g, w_pool, pool_scale,
           w_out_ab, w_qkv, rel_bias, w_o, w_up, conv_w, conv_b, w_down):
    bsz, seq, d = x.shape
    depth = w_ada.shape[0]
    alpha = (2.0 * depth) ** 0.25
    fpad = FFN_PAD - FFN_DIM

    mod = _ada_modulation(c, w_ada, b_ada).reshape(depth, bsz, 6, 1, d)

    def mod_vec(layer, idx):
        return mod[layer, :, idx]

    u = _modulate(x, mod_vec(0, 1), mod_vec(0, 0))
    for layer in range(depth):
        if layer % 2 == 0:
            e = layer // 2
            w_main = w_in_ab[e, :, :OFF_LR].astype(BF16)
            w_lr = jnp.pad(w_in_ab[e, :, OFF_LR:], ((0, 0), (0, LANES - GLA_GATE_RANK))).astype(BF16)
            w_gate = jnp.pad(w_gate_lr[e], ((0, LANES - GLA_GATE_RANK), (0, 0))).astype(BF16)
            u2d = u.reshape(bsz * seq, d)
            z = _matmul(u2d, w_main, "in_proj").reshape(bsz, seq, OFF_LR)
            log_a = _gla_gate(u2d, w_lr, w_gate, b_gate[e][None]).reshape(bsz, seq, GLA_HEADS * GLA_DK)
            y = _even_mix(z, log_a, w_pool[e].astype(BF16), pool_scale[e][None], gla_norm_g[e][None])
            w_proj = w_out_ab[e].astype(BF16)
        else:
            o = layer // 2
            qkv = _matmul(u.reshape(bsz * seq, d), w_qkv[o].astype(BF16), "qkv_proj").reshape(bsz, seq, 3 * d)
            y = _band_attention(qkv, _attention_bias_table(rel_bias[o]))
            w_proj = w_o[o].astype(BF16)
        x, u = _mm_ln(y, w_proj, x, mod_vec(layer, 2), ln_g[layer, 0][None], ln_b[layer, 0][None],
                      mod_vec(layer, 4), mod_vec(layer, 3), alpha, 512, "mixer_out_ln")

        w_up_pad = jnp.concatenate([
            jnp.pad(w_up[layer, :, :FFN_DIM], ((0, 0), (0, fpad))),
            jnp.pad(w_up[layer, :, FFN_DIM:], ((0, 0), (0, fpad)))], axis=1).astype(BF16)
        h = _ffn_up(u, w_up_pad, jnp.pad(conv_w[layer], ((0, 0), (0, fpad))),
                    jnp.pad(conv_b[layer], (0, fpad))[None])
        w_down_pad = jnp.pad(w_down[layer], ((0, fpad), (0, 0))).astype(BF16)
        last = layer == depth - 1
        x, u = _mm_ln(h, w_down_pad, x, mod_vec(layer, 5), ln_g[layer, 1][None], ln_b[layer, 1][None],
                      None if last else mod_vec(layer + 1, 1), None if last else mod_vec(layer + 1, 0),
                      alpha, 256, "ffn_down_ln")
    return x
```

```python
import functools

import jax
import jax.numpy as jnp
from jax import lax
from jax.experimental import pallas as pl
from jax.experimental.pallas import tpu as pltpu

F32 = jnp.float32
BF16 = jnp.bfloat16

D_MODEL = 2048
CHUNK = 64
POOL_WIDTH = D_MODEL // 2
POOL_WINDOWS = (2, 4, 8, 16)
POOL_GROUP = POOL_WIDTH // len(POOL_WINDOWS)
GLA_HEADS = 4
GLA_DV = (D_MODEL // 2) // GLA_HEADS
GLA_DK = GLA_DV // 2
GLA_GATE_RANK = 16
GLA_TAU = 16.0
OFF_Q = POOL_WIDTH
OFF_K = OFF_Q + GLA_HEADS * GLA_DK
OFF_V = OFF_K + GLA_HEADS * GLA_DK
OFF_R = OFF_V + GLA_HEADS * GLA_DV
OFF_LR = OFF_R + GLA_HEADS * GLA_DV
ATT_HEADS = 16
ATT_HEAD_DIM = D_MODEL // ATT_HEADS
ATT_PAST_CHUNKS = 8
REL_CLIP = 256
FFN_DIM = 5504
LN_EPS = 1e-5
NEG_INF = -1e30

V7X_VMEM_BYTES = 64 * 2**20
LANES = 128
F32_SUBLANES = 8
BF16_SUBLANES = 16
CONV_TAIL = F32_SUBLANES

FFN_TILE = 512
FFN_PAD = -(-FFN_DIM // FFN_TILE) * FFN_TILE
HALO = BF16_SUBLANES
ATT_Q = 4 * CHUNK
ATT_WIN = ATT_Q + ATT_PAST_CHUNKS * CHUNK
ATT_HB = 8
ATT_KEY_BLOCKS = ATT_WIN // ATT_Q
ATT_DIAG = ATT_WIN + ATT_Q
LOG2_E = 1.4426950408889634


def _cparams(semantics, vmem_bytes):
    limit = min(int(vmem_bytes), V7X_VMEM_BYTES - 4 * 2**20)
    return pltpu.CompilerParams(dimension_semantics=semantics, vmem_limit_bytes=limit)


def _nbytes(shape, dtype):
    n = 1
    for s in shape:
        n *= s
    return n * jnp.dtype(dtype).itemsize


def _ada_kernel(c_ref, w_ref, b_ref, o_ref):
    c = c_ref[...]
    cond = (c * jax.nn.sigmoid(c)).astype(BF16)
    o_ref[0] = jnp.dot(cond, w_ref[0].astype(BF16), preferred_element_type=F32) + b_ref[0]


def _ada_modulation(c, w_ada, b_ada):
    depth, d, n = w_ada.shape
    bsz = c.shape[0]
    rows = -(-bsz // 8) * 8
    bn = 1024
    c_pad = jnp.pad(c, ((0, rows - bsz), (0, 0)))
    vmem = 2 * (_nbytes((d, bn), F32) + _nbytes((rows, bn), F32)) + _nbytes((d, bn), BF16) + 2**20
    out = pl.pallas_call(
        _ada_kernel,
        grid=(depth, n // bn),
        in_specs=[
            pl.BlockSpec((rows, d), lambda l, j: (0, 0)),
            pl.BlockSpec((1, d, bn), lambda l, j: (l, 0, j)),
            pl.BlockSpec((1, 1, bn), lambda l, j: (l, 0, j)),
        ],
        out_specs=pl.BlockSpec((1, rows, bn), lambda l, j: (l, 0, j)),
        out_shape=jax.ShapeDtypeStruct((depth, rows, n), F32),
        compiler_params=_cparams(("arbitrary", "arbitrary"), vmem),
        name="ada_modulation",
    )(c_pad, w_ada, b_ada.reshape(depth, 1, n))
    return out[:, :bsz]


def _modulate_kernel(x_ref, sc_ref, sh_ref, u_ref):
    u_ref[0] = (x_ref[0] * (1.0 + sc_ref[0]) + sh_ref[0]).astype(BF16)


def _modulate(x, sc, sh):
    bsz, seq, d = x.shape
    bs = min(1024, seq)
    vmem = 2 * (_nbytes((bs, d), F32) + _nbytes((bs, d), BF16)) + 2**20
    return pl.pallas_call(
        _modulate_kernel,
        grid=(bsz, seq // bs),
        in_specs=[
            pl.BlockSpec((1, bs, d), lambda b, s: (b, s, 0)),
            pl.BlockSpec((1, 1, d), lambda b, s: (b, 0, 0)),
            pl.BlockSpec((1, 1, d), lambda b, s: (b, 0, 0)),
        ],
        out_specs=pl.BlockSpec((1, bs, d), lambda b, s: (b, s, 0)),
        out_shape=jax.ShapeDtypeStruct((bsz, seq, d), BF16),
        compiler_params=_cparams(("arbitrary", "arbitrary"), vmem),
        name="modulate",
    )(x, sc, sh)


def _matmul_kernel(a_ref, w_ref, o_ref):
    o_ref[...] = jnp.dot(a_ref[...], w_ref[...], preferred_element_type=F32).astype(o_ref.dtype)


def _matmul(a, w, name):
    m, k = a.shape
    n = w.shape[1]
    bm = min(1024, m)
    bn = min(2048, n)
    vmem = (2 * (_nbytes((bm, k), BF16) + _nbytes((k, bn), BF16) + _nbytes((bm, bn), BF16))
            + _nbytes((bm, bn), F32) + 2**20)
    return pl.pallas_call(
        _matmul_kernel,
        grid=(m // bm, n // bn),
        in_specs=[
            pl.BlockSpec((bm, k), lambda i, j: (i, 0)),
            pl.BlockSpec((k, bn), lambda i, j: (0, j)),
        ],
        out_specs=pl.BlockSpec((bm, bn), lambda i, j: (i, j)),
        out_shape=jax.ShapeDtypeStruct((m, n), BF16),
        compiler_params=_cparams(("arbitrary", "arbitrary"), vmem),
        name=name,
    )(a, w)


def _gate_kernel(u_ref, wlr_ref, wg_ref, bg_ref, o_ref):
    z_lr = jnp.dot(u_ref[...], wlr_ref[...], preferred_element_type=F32).astype(BF16)
    pre = jnp.dot(z_lr, wg_ref[...], preferred_element_type=F32) + bg_ref[...]
    log_sig = jnp.minimum(pre, 0.0) - jnp.log1p(jnp.exp(-jnp.abs(pre)))
    o_ref[...] = log_sig * (1.0 / GLA_TAU)


def _gla_gate(u2d, w_lr, w_gate, b_gate):
    m, k = u2d.shape
    n = w_gate.shape[1]
    bm = min(1024, m)
    vmem = 2 * (_nbytes((bm, k), BF16) + _nbytes((k, LANES), BF16) + _nbytes((bm, n), F32)) + 4 * 2**20
    return pl.pallas_call(
        _gate_kernel,
        grid=(m // bm,),
        in_specs=[
            pl.BlockSpec((bm, k), lambda i: (i, 0)),
            pl.BlockSpec((k, LANES), lambda i: (0, 0)),
            pl.BlockSpec((LANES, n), lambda i: (0, 0)),
            pl.BlockSpec((1, n), lambda i: (0, 0)),
        ],
        out_specs=pl.BlockSpec((bm, n), lambda i: (i, 0)),
        out_shape=jax.ShapeDtypeStruct((m, n), F32),
        compiler_params=_cparams(("arbitrary",), vmem),
        name="gla_gate",
    )(u2d, w_lr, w_gate, b_gate)


def _even_mix_kernel(z_ref, halo_ref, la_ref, wpool_ref, pscale_ref, gng_ref, o_ref, state_ref, *, rows):
    t = pl.program_id(1)
    first = t == 0
    n_chunks = rows // CHUNK

    halo = jnp.where(first, 0.0, halo_ref[0].astype(F32))
    ext = jnp.concatenate([halo, z_ref[0, :, 0:POOL_WIDTH].astype(F32)], axis=0)
    grow = lax.broadcasted_iota(jnp.int32, (rows, 1), 0) + t * rows
    for gi, w in enumerate(POOL_WINDOWS):
        cols = slice(gi * POOL_GROUP, (gi + 1) * POOL_GROUP)
        e = ext[:, cols]
        s = e
        step = 1
        while step < w:
            s = s + pltpu.roll(s, step, 0)
            step *= 2
        count = jnp.minimum(grow + 1, w).astype(F32)
        dev = s[HALO:] / count - e[HALO:]
        y_a = jnp.dot(dev.astype(BF16), wpool_ref[gi], preferred_element_type=F32) * pscale_ref[:, cols]
        o_ref[0, :, cols] = y_a.astype(BF16)

    @pl.when(first)
    def _():
        state_ref[...] = jnp.zeros_like(state_ref)

    in_chunk = lax.broadcasted_iota(jnp.int32, (rows, 1), 0) % CHUNK
    b_all = la_ref[0]
    step = 1
    while step < CHUNK:
        b_all = b_all + jnp.where(in_chunk >= step, pltpu.roll(b_all, step, 0), 0.0)
        step *= 2
    q_all = z_ref[0, :, OFF_Q:OFF_K].astype(F32)
    k_all = z_ref[0, :, OFF_K:OFF_V].astype(F32)
    q_dec = ((q_all * (GLA_DK ** -0.5)) * jnp.exp(b_all)).astype(BF16)
    k_inv = (k_all * jnp.exp(-b_all)).astype(BF16)
    causal = (lax.broadcasted_iota(jnp.int32, (CHUNK, CHUNK), 0)
              >= lax.broadcasted_iota(jnp.int32, (CHUNK, CHUNK), 1))
    states = [state_ref[h] for h in range(GLA_HEADS)]
    for c in range(n_chunks):
        rs = slice(c * CHUNK, (c + 1) * CHUNK)
        b_c = b_all[rs]
        b_last = b_c[CHUNK - 1:CHUNK]
        k_dec = (k_all[rs] * jnp.exp(b_last - b_c)).astype(BF16)
        decay = jnp.exp(b_last)
        v_c = z_ref[0, rs, OFF_V:OFF_R]
        r_c = z_ref[0, rs, OFF_R:OFF_LR].astype(F32)
        for h in range(GLA_HEADS):
            ks = slice(h * GLA_DK, (h + 1) * GLA_DK)
            vs = slice(h * GLA_DV, (h + 1) * GLA_DV)
            qh = q_dec[rs, ks]
            vh = v_c[:, vs]
            scores = lax.dot_general(qh, k_inv[rs, ks], (((1,), (1,)), ((), ())),
                                     preferred_element_type=F32)
            scores = jnp.where(causal, scores, 0.0).astype(BF16)
            o = jnp.dot(scores, vh, preferred_element_type=F32)
            o = o + lax.dot_general(qh, states[h].astype(BF16), (((1,), (1,)), ((), ())),
                                    preferred_element_type=F32)
            update = lax.dot_general(vh, k_dec[:, ks], (((0,), (0,)), ((), ())),
                                     preferred_element_type=F32)
            states[h] = decay[:, ks] * states[h] + update
            mu = jnp.mean(o, axis=-1, keepdims=True)
            oc = o - mu
            var = jnp.mean(oc * oc, axis=-1, keepdims=True)
            o_n = (oc * lax.rsqrt(var + LN_EPS)) * gng_ref[:, vs]
            r_h = r_c[:, vs]
            y_b = (r_h * jax.nn.sigmoid(r_h)) * o_n
            o_ref[0, rs, POOL_WIDTH + h * GLA_DV:POOL_WIDTH + (h + 1) * GLA_DV] = y_b.astype(BF16)
    for h in range(GLA_HEADS):
        state_ref[h] = states[h]


def _even_mix(z, log_a, w_pool, pool_scale, gla_norm_g):
    bsz, seq, zw = z.shape
    rows = min(256, seq)
    hk = GLA_HEADS * GLA_DK
    halo_blocks = rows // HALO
    vmem = (2 * (_nbytes((rows, zw), BF16) + _nbytes((rows, hk), F32) + _nbytes((rows, D_MODEL), BF16)
                 + _nbytes((4, POOL_GROUP, POOL_GROUP), BF16))
            + 12 * _nbytes((rows, POOL_WIDTH), F32) + 4 * 2**20)
    return pl.pallas_call(
        functools.partial(_even_mix_kernel, rows=rows),
        grid=(bsz, seq // rows),
        in_specs=[
            pl.BlockSpec((1, rows, zw), lambda b, t: (b, t, 0)),
            pl.BlockSpec((1, HALO, POOL_WIDTH), lambda b, t: (b, jnp.maximum(t * halo_blocks - 1, 0), 0)),
            pl.BlockSpec((1, rows, hk), lambda b, t: (b, t, 0)),
            pl.BlockSpec((len(POOL_WINDOWS), POOL_GROUP, POOL_GROUP), lambda b, t: (0, 0, 0)),
            pl.BlockSpec((1, POOL_WIDTH), lambda b, t: (0, 0)),
            pl.BlockSpec((1, GLA_HEADS * GLA_DV), lambda b, t: (0, 0)),
        ],
        out_specs=pl.BlockSpec((1, rows, D_MODEL), lambda b, t: (b, t, 0)),
        out_shape=jax.ShapeDtypeStruct((bsz, seq, D_MODEL), BF16),
        scratch_shapes=[pltpu.VMEM((GLA_HEADS, GLA_DV, GLA_DK), F32)],
        compiler_params=_cparams(("arbitrary", "arbitrary"), vmem),
        name="even_mix",
    )(z, z, log_a, w_pool, pool_scale, gla_norm_g)


def _attn_kernel(diag_ref, q_ref, k0_ref, k1_ref, k2_ref, v0_ref, v1_ref, v2_ref, o_ref, bias_ref):
    hb = diag_ref.shape[0]

    @pl.when((pl.program_id(1) == 0) & (pl.program_id(2) == 0))
    def _():
        qpos = lax.broadcasted_iota(jnp.int32, (ATT_Q, ATT_WIN), 0)
        kpos = lax.broadcasted_iota(jnp.int32, (ATT_Q, ATT_WIN), 1)
        lag = kpos // CHUNK - qpos // CHUNK
        in_band = (lag >= 0) & (lag <= ATT_PAST_CHUNKS)
        for h in range(hb):
            rows = jnp.broadcast_to(diag_ref[h], (ATT_Q, ATT_DIAG))
            table = pltpu.roll(rows, 0, 1, stride=1, stride_axis=0)[:, :ATT_WIN] * LOG2_E
            table = jnp.where(in_band, table, NEG_INF)
            for step in range(ATT_KEY_BLOCKS):
                lead = (ATT_KEY_BLOCKS - 1 - step) * ATT_Q
                bias_ref[step, h] = jnp.where(kpos >= lead, table, NEG_INF)

    variant = jnp.minimum(pl.program_id(2), ATT_KEY_BLOCKS - 1)
    k_refs = (k0_ref, k1_ref, k2_ref)
    v_refs = (v0_ref, v1_ref, v2_ref)
    nt = (((1,), (1,)), ((), ()))
    def head_cols(h):
        return slice(h * ATT_HEAD_DIM, (h + 1) * ATT_HEAD_DIM)

    def qk(h):
        q = q_ref[0, :, head_cols(h)]
        return jnp.concatenate(
            [lax.dot_general(q, kr[0, :, head_cols(h)], nt, preferred_element_type=F32) for kr in k_refs],
            axis=1)

    def softmax(h, s):
        s = s + bias_ref[variant, h]
        p = jnp.exp2(s - jnp.max(s, axis=-1, keepdims=True))
        return p.astype(BF16), jnp.sum(p, axis=-1, keepdims=True)

    def pv(h, p, denom):
        o = jnp.dot(p[:, 0:ATT_Q], v_refs[0][0, :, head_cols(h)], preferred_element_type=F32)
        for j in range(1, ATT_KEY_BLOCKS):
            o = o + jnp.dot(p[:, j * ATT_Q:(j + 1) * ATT_Q], v_refs[j][0, :, head_cols(h)],
                            preferred_element_type=F32)
        o_ref[0, :, head_cols(h)] = (o / denom).astype(BF16)

    scores, probs = {}, {}
    for t in range(hb + 2):
        if t < hb:
            scores[t] = qk(t)
        if 0 <= t - 1 < hb:
            probs[t - 1] = softmax(t - 1, scores.pop(t - 1))
        if 0 <= t - 2 < hb:
            pv(t - 2, *probs.pop(t - 2))


def _band_attention(qkv, diag):
    bsz, seq, _ = qkv.shape
    hb = ATT_HB
    width = hb * ATT_HEAD_DIM
    groups = ATT_HEADS // hb
    blk = (1, ATT_Q, width)

    def key_spec(col0, back):
        return pl.BlockSpec(blk, lambda g, b, i: (b, jnp.maximum(i - back, 0), col0 + g))

    vmem = (2 * 8 * _nbytes(blk, BF16) + _nbytes((ATT_KEY_BLOCKS, hb, ATT_Q, ATT_WIN), F32)
            + 8 * _nbytes((ATT_Q, ATT_WIN), F32) + 4 * 2**20)
    return pl.pallas_call(
        _attn_kernel,
        grid=(groups, bsz, seq // ATT_Q),
        in_specs=[
            pl.BlockSpec((hb, 1, ATT_DIAG), lambda g, b, i: (g, 0, 0)),
            pl.BlockSpec(blk, lambda g, b, i: (b, i, g)),
            key_spec(groups, 2), key_spec(groups, 1), key_spec(groups, 0),
            key_spec(2 * groups, 2), key_spec(2 * groups, 1), key_spec(2 * groups, 0),
        ],
        out_specs=pl.BlockSpec(blk, lambda g, b, i: (b, i, g)),
        out_shape=jax.ShapeDtypeStruct((bsz, seq, D_MODEL), BF16),
        scratch_shapes=[pltpu.VMEM((ATT_KEY_BLOCKS, hb, ATT_Q, ATT_WIN), F32)],
        compiler_params=_cparams(("arbitrary", "arbitrary", "arbitrary"), vmem),
        name="band_attention",
    )(diag, qkv, qkv, qkv, qkv, qkv, qkv, qkv)


def _attention_bias_diagonal(rel_bias):
    pad = ATT_PAST_CHUNKS * CHUNK
    m = jnp.arange(ATT_DIAG)
    q_minus_k = jnp.where(m < ATT_WIN, -m, ATT_DIAG - m)
    rel = jnp.clip(pad + q_minus_k, -REL_CLIP, REL_CLIP) + REL_CLIP
    return rel_bias[:, rel].astype(F32)[:, None, :]


def _mm_ln_kernel(a_ref, w_ref, x_ref, gate_ref, lng_ref, lnb_ref, *rest, alpha, emit_u, sub):
    if emit_u:
        sc_ref, sh_ref, xo_ref, uo_ref = rest
    else:
        (xo_ref,) = rest
    for r in range(a_ref.shape[1] // sub):
        rows = slice(r * sub, (r + 1) * sub)
        y = jnp.dot(a_ref[0, rows, :], w_ref[...], preferred_element_type=F32)
        t = alpha * x_ref[0, rows, :] + (1.0 + gate_ref[0]) * y
        mu = jnp.mean(t, axis=-1, keepdims=True)
        tc = t - mu
        var = jnp.mean(tc * tc, axis=-1, keepdims=True)
        xn = (tc * lax.rsqrt(var + LN_EPS)) * lng_ref[...] + lnb_ref[...]
        xo_ref[0, rows, :] = xn
        if emit_u:
            uo_ref[0, rows, :] = (xn * (1.0 + sc_ref[0]) + sh_ref[0]).astype(BF16)


def _mm_ln(a, w, x, gate, ln_g, ln_b, next_sc, next_sh, alpha, bm, name):
    bsz, seq, kdim = a.shape
    d = w.shape[1]
    bm = min(bm, seq)
    sub = min(256, bm)
    emit_u = next_sc is not None
    vec = pl.BlockSpec((1, 1, d), lambda b, i: (b, 0, 0))
    par = pl.BlockSpec((1, d), lambda b, i: (0, 0))
    row = pl.BlockSpec((1, bm, d), lambda b, i: (b, i, 0))
    in_specs = [
        pl.BlockSpec((1, bm, kdim), lambda b, i: (b, i, 0)),
        pl.BlockSpec((kdim, d), lambda b, i: (0, 0), pipeline_mode=pl.Buffered(1)),
        row, vec, par, par,
    ]
    args = [a, w, x, gate, ln_g, ln_b]
    out_specs = [row]
    out_shape = [jax.ShapeDtypeStruct((bsz, seq, d), F32)]
    if emit_u:
        in_specs += [vec, vec]
        args += [next_sc, next_sh]
        out_specs.append(row)
        out_shape.append(jax.ShapeDtypeStruct((bsz, seq, d), BF16))
    vmem = (_nbytes((kdim, d), BF16)
            + 2 * (_nbytes((bm, kdim), BF16) + 2 * _nbytes((bm, d), F32) + _nbytes((bm, d), BF16))
            + 6 * _nbytes((sub, d), F32) + 2 * 2**20)
    outs = pl.pallas_call(
        functools.partial(_mm_ln_kernel, alpha=alpha, emit_u=emit_u, sub=sub),
        grid=(bsz, seq // bm),
        in_specs=in_specs,
        out_specs=out_specs,
        out_shape=out_shape,
        compiler_params=_cparams(("arbitrary", "arbitrary"), vmem),
        name=name,
    )(*args)
    return (outs[0], outs[1]) if emit_u else (outs[0], None)


def _ffn_up_kernel(u_ref, wa_ref, wg_ref, cw_ref, cb_ref, h_ref, tail_ref):
    @pl.when(pl.program_id(2) == 0)
    def _():
        tail_ref[...] = jnp.zeros(tail_ref.shape, F32)

    u = u_ref[0]
    rows = u.shape[0]
    tail = tail_ref[...]
    prev1 = tail[CONV_TAIL - 1:CONV_TAIL]
    prev2 = tail[CONV_TAIL - 2:CONV_TAIL - 1]
    g = jnp.dot(u, wg_ref[...], preferred_element_type=F32)
    row = lax.broadcasted_iota(jnp.int32, (rows, 1), 0)
    g1 = jnp.where(row == 0, prev1, pltpu.roll(g, 1, 0))
    g2 = jnp.where(row == 0, prev2, jnp.where(row == 1, prev1, pltpu.roll(g, 2, 0)))
    gate = jax.nn.gelu(cb_ref[...] + g2 * cw_ref[0:1] + g1 * cw_ref[1:2] + g * cw_ref[2:3])
    a = jnp.dot(u, wa_ref[...], preferred_element_type=F32)
    h_ref[0] = (gate * a).astype(BF16)
    tail_ref[...] = g[rows - CONV_TAIL:rows]


def _ffn_up(u, w_up, conv_w, conv_b):
    bsz, seq, d = u.shape
    bm = min(1024, seq)
    bn = FFN_TILE
    n_tiles = FFN_PAD // bn
    vmem = (2 * (_nbytes((bm, d), BF16) + 2 * _nbytes((d, bn), BF16) + _nbytes((bm, bn), BF16))
            + 6 * _nbytes((bm, bn), F32) + 2 * 2**20)
    return pl.pallas_call(
        _ffn_up_kernel,
        grid=(bsz, n_tiles, seq // bm),
        in_specs=[
            pl.BlockSpec((1, bm, d), lambda b, j, i: (b, i, 0)),
            pl.BlockSpec((d, bn), lambda b, j, i: (0, j)),
            pl.BlockSpec((d, bn), lambda b, j, i: (0, n_tiles + j)),
            pl.BlockSpec((3, bn), lambda b, j, i: (0, j)),
            pl.BlockSpec((1, bn), lambda b, j, i: (0, j)),
        ],
        out_specs=pl.BlockSpec((1, bm, bn), lambda b, j, i: (b, i, j)),
        out_shape=jax.ShapeDtypeStruct((bsz, seq, FFN_PAD), BF16),
        scratch_shapes=[pltpu.VMEM((CONV_TAIL, bn), F32)],
        compiler_params=_cparams(("arbitrary", "arbitrary", "arbitrary"), vmem),
        name="ffn_up",
    )(u, w_up, w_up, conv_w, conv_b)


def kernel(x, c, w_ada, b_ada, ln_g, ln_b, w_in_ab, w_gate_lr, b_gate, gla_norm_g, w_pool, pool_scale,
           w_out_ab, w_qkv, rel_bias, w_o, w_up, conv_w, conv_b, w_down):
    bsz, seq, d = x.shape
    depth = w_ada.shape[0]
    alpha = (2.0 * depth) ** 0.25
    fpad = FFN_PAD - FFN_DIM

    mod = _ada_modulation(c, w_ada, b_ada).reshape(depth, bsz, 6, 1, d)

    def mod_vec(layer, idx):
        return mod[layer, :, idx]

    u = _modulate(x, mod_vec(0, 1), mod_vec(0, 0))
    for layer in range(depth):
        if layer % 2 == 0:
            e = layer // 2
            w_main = w_in_ab[e, :, :OFF_LR].astype(BF16)
            w_lr = jnp.pad(w_in_ab[e, :, OFF_LR:], ((0, 0), (0, LANES - GLA_GATE_RANK))).astype(BF16)
            w_gate = jnp.pad(w_gate_lr[e], ((0, LANES - GLA_GATE_RANK), (0, 0))).astype(BF16)
            u2d = u.reshape(bsz * seq, d)
            z = _matmul(u2d, w_main, "in_proj").reshape(bsz, seq, OFF_LR)
            log_a = _gla_gate(u2d, w_lr, w_gate, b_gate[e][None]).reshape(bsz, seq, GLA_HEADS * GLA_DK)
            y = _even_mix(z, log_a, w_pool[e].astype(BF16), pool_scale[e][None], gla_norm_g[e][None])
            w_proj = w_out_ab[e].astype(BF16)
        else:
            o = layer // 2
            col_scale = jnp.where(jnp.arange(3 * d) < d, ATT_HEAD_DIM ** -0.5 * LOG2_E, 1.0).astype(F32)
            w_att = (w_qkv[o] * col_scale[None, :]).astype(BF16)
            qkv = _matmul(u.reshape(bsz * seq, d), w_att, "qkv_proj").reshape(bsz, seq, 3 * d)
            y = _band_attention(qkv, _attention_bias_diagonal(rel_bias[o]))
            w_proj = w_o[o].astype(BF16)
        x, u = _mm_ln(y, w_proj, x, mod_vec(layer, 2), ln_g[layer, 0][None], ln_b[layer, 0][None],
                      mod_vec(layer, 4), mod_vec(layer, 3), alpha, 512, "mixer_out_ln")

        w_up_pad = jnp.concatenate([
            jnp.pad(w_up[layer, :, :FFN_DIM], ((0, 0), (0, fpad))),
            jnp.pad(w_up[layer, :, FFN_DIM:], ((0, 0), (0, fpad)))], axis=1).astype(BF16)
        h = _ffn_up(u, w_up_pad, jnp.pad(conv_w[layer], ((0, 0), (0, fpad))),
                    jnp.pad(conv_b[layer], (0, fpad))[None])
        w_down_pad = jnp.pad(w_down[layer], ((0, fpad), (0, 0))).astype(BF16)
        last = layer == depth - 1
        x, u = _mm_ln(h, w_down_pad, x, mod_vec(layer, 5), ln_g[layer, 1][None], ln_b[layer, 1][None],
                      None if last else mod_vec(layer + 1, 1), None if last else mod_vec(layer + 1, 0),
                      alpha, 256, "ffn_down_ln")
    return x
```

```python
import functools

import jax
import jax.numpy as jnp
from jax import lax
from jax.experimental import pallas as pl
from jax.experimental.pallas import tpu as pltpu

F32 = jnp.float32
BF16 = jnp.bfloat16

D_MODEL = 2048
CHUNK = 64
POOL_WIDTH = D_MODEL // 2
POOL_WINDOWS = (2, 4, 8, 16)
POOL_GROUP = POOL_WIDTH // len(POOL_WINDOWS)
GLA_HEADS = 4
GLA_DV = (D_MODEL // 2) // GLA_HEADS
GLA_DK = GLA_DV // 2
GLA_GATE_RANK = 16
GLA_TAU = 16.0
OFF_Q = POOL_WIDTH
OFF_K = OFF_Q + GLA_HEADS * GLA_DK
OFF_V = OFF_K + GLA_HEADS * GLA_DK
OFF_R = OFF_V + GLA_HEADS * GLA_DV
OFF_LR = OFF_R + GLA_HEADS * GLA_DV
ATT_HEADS = 16
ATT_HEAD_DIM = D_MODEL // ATT_HEADS
ATT_PAST_CHUNKS = 8
REL_CLIP = 256
FFN_DIM = 5504
LN_EPS = 1e-5
NEG_INF = -1e30

V7X_VMEM_BYTES = 64 * 2**20
LANES = 128
F32_SUBLANES = 8
BF16_SUBLANES = 16
CONV_TAIL = F32_SUBLANES

FFN_TILE = 512
FFN_PAD = -(-FFN_DIM // FFN_TILE) * FFN_TILE
HALO = BF16_SUBLANES
ATT_Q = 4 * CHUNK
ATT_WIN = ATT_Q + ATT_PAST_CHUNKS * CHUNK
ATT_HB = 8
ATT_KEY_BLOCKS = ATT_WIN // ATT_Q
ATT_DIAG = ATT_WIN + ATT_Q
LOG2_E = 1.4426950408889634


def _cparams(semantics, vmem_bytes):
    limit = min(int(vmem_bytes), V7X_VMEM_BYTES - 4 * 2**20)
    return pltpu.CompilerParams(dimension_semantics=semantics, vmem_limit_bytes=limit)


def _nbytes(shape, dtype):
    n = 1
    for s in shape:
        n *= s
    return n * jnp.dtype(dtype).itemsize


def _ada_kernel(c_ref, w_ref, b_ref, o_ref):
    c = c_ref[...]
    cond = (c * jax.nn.sigmoid(c)).astype(BF16)
    o_ref[0] = jnp.dot(cond, w_ref[0].astype(BF16), preferred_element_type=F32) + b_ref[0]


def _ada_modulation(c, w_ada, b_ada):
    depth, d, n = w_ada.shape
    bsz = c.shape[0]
    rows = -(-bsz // 8) * 8
    bn = 1024
    c_pad = jnp.pad(c, ((0, rows - bsz), (0, 0)))
    vmem = 2 * (_nbytes((d, bn), F32) + _nbytes((rows, bn), F32)) + _nbytes((d, bn), BF16) + 2**20
    out = pl.pallas_call(
        _ada_kernel,
        grid=(depth, n // bn),
        in_specs=[
            pl.BlockSpec((rows, d), lambda l, j: (0, 0)),
            pl.BlockSpec((1, d, bn), lambda l, j: (l, 0, j)),
            pl.BlockSpec((1, 1, bn), lambda l, j: (l, 0, j)),
        ],
        out_specs=pl.BlockSpec((1, rows, bn), lambda l, j: (l, 0, j)),
        out_shape=jax.ShapeDtypeStruct((depth, rows, n), F32),
        compiler_params=_cparams(("arbitrary", "arbitrary"), vmem),
        name="ada_modulation",
    )(c_pad, w_ada, b_ada.reshape(depth, 1, n))
    return out[:, :bsz]


def _modulate_kernel(x_ref, sc_ref, sh_ref, u_ref):
    u_ref[0] = (x_ref[0] * (1.0 + sc_ref[0]) + sh_ref[0]).astype(BF16)


def _modulate(x, sc, sh):
    bsz, seq, d = x.shape
    bs = min(1024, seq)
    vmem = 2 * (_nbytes((bs, d), F32) + _nbytes((bs, d), BF16)) + 2**20
    return pl.pallas_call(
        _modulate_kernel,
        grid=(bsz, seq // bs),
        in_specs=[
            pl.BlockSpec((1, bs, d), lambda b, s: (b, s, 0)),
            pl.BlockSpec((1, 1, d), lambda b, s: (b, 0, 0)),
            pl.BlockSpec((1, 1, d), lambda b, s: (b, 0, 0)),
        ],
        out_specs=pl.BlockSpec((1, bs, d), lambda b, s: (b, s, 0)),
        out_shape=jax.ShapeDtypeStruct((bsz, seq, d), BF16),
        compiler_params=_cparams(("arbitrary", "arbitrary"), vmem),
        name="modulate",
    )(x, sc, sh)


def _matmul_kernel(a_ref, w_ref, o_ref):
    o_ref[...] = jnp.dot(a_ref[...], w_ref[...], preferred_element_type=F32).astype(o_ref.dtype)


def _matmul(a, w, name):
    m, k = a.shape
    n = w.shape[1]
    bm = min(1024, m)
    bn = min(2048, n)
    vmem = (2 * (_nbytes((bm, k), BF16) + _nbytes((k, bn), BF16) + _nbytes((bm, bn), BF16))
            + _nbytes((bm, bn), F32) + 2**20)
    return pl.pallas_call(
        _matmul_kernel,
        grid=(m // bm, n // bn),
        in_specs=[
            pl.BlockSpec((bm, k), lambda i, j: (i, 0)),
            pl.BlockSpec((k, bn), lambda i, j: (0, j)),
        ],
        out_specs=pl.BlockSpec((bm, bn), lambda i, j: (i, j)),
        out_shape=jax.ShapeDtypeStruct((m, n), BF16),
        compiler_params=_cparams(("arbitrary", "arbitrary"), vmem),
        name=name,
    )(a, w)


def _gate_kernel(u_ref, wlr_ref, wg_ref, bg_ref, o_ref):
    z_lr = jnp.dot(u_ref[...], wlr_ref[...], preferred_element_type=F32).astype(BF16)
    pre = jnp.dot(z_lr, wg_ref[...], preferred_element_type=F32) + bg_ref[...]
    log_sig = jnp.minimum(pre, 0.0) - jnp.log1p(jnp.exp(-jnp.abs(pre)))
    o_ref[...] = log_sig * (1.0 / GLA_TAU)


def _gla_gate(u2d, w_lr, w_gate, b_gate):
    m, k = u2d.shape
    n = w_gate.shape[1]
    bm = min(1024, m)
    vmem = 2 * (_nbytes((bm, k), BF16) + _nbytes((k, LANES), BF16) + _nbytes((bm, n), F32)) + 4 * 2**20
    return pl.pallas_call(
        _gate_kernel,
        grid=(m // bm,),
        in_specs=[
            pl.BlockSpec((bm, k), lambda i: (i, 0)),
            pl.BlockSpec((k, LANES), lambda i: (0, 0)),
            pl.BlockSpec((LANES, n), lambda i: (0, 0)),
            pl.BlockSpec((1, n), lambda i: (0, 0)),
        ],
        out_specs=pl.BlockSpec((bm, n), lambda i: (i, 0)),
        out_shape=jax.ShapeDtypeStruct((m, n), F32),
        compiler_params=_cparams(("arbitrary",), vmem),
        name="gla_gate",
    )(u2d, w_lr, w_gate, b_gate)


def _even_mix_kernel(z_ref, halo_ref, la_ref, wpool_ref, pscale_ref, gng_ref, o_ref, state_ref, *, rows):
    t = pl.program_id(1)
    first = t == 0
    n_chunks = rows // CHUNK

    halo = jnp.where(first, 0.0, halo_ref[0].astype(F32))
    ext = jnp.concatenate([halo, z_ref[0, :, 0:POOL_WIDTH].astype(F32)], axis=0)
    grow = lax.broadcasted_iota(jnp.int32, (rows, 1), 0) + t * rows
    for gi, w in enumerate(POOL_WINDOWS):
        cols = slice(gi * POOL_GROUP, (gi + 1) * POOL_GROUP)
        e = ext[:, cols]
        s = e
        step = 1
        while step < w:
            s = s + pltpu.roll(s, step, 0)
            step *= 2
        count = jnp.minimum(grow + 1, w).astype(F32)
        dev = s[HALO:] / count - e[HALO:]
        y_a = jnp.dot(dev.astype(BF16), wpool_ref[gi], preferred_element_type=F32) * pscale_ref[:, cols]
        o_ref[0, :, cols] = y_a.astype(BF16)

    @pl.when(first)
    def _():
        state_ref[...] = jnp.zeros_like(state_ref)

    in_chunk = lax.broadcasted_iota(jnp.int32, (rows, 1), 0) % CHUNK
    b_all = la_ref[0]
    step = 1
    while step < CHUNK:
        b_all = b_all + jnp.where(in_chunk >= step, pltpu.roll(b_all, step, 0), 0.0)
        step *= 2
    q_all = z_ref[0, :, OFF_Q:OFF_K].astype(F32)
    k_all = z_ref[0, :, OFF_K:OFF_V].astype(F32)
    q_dec = ((q_all * (GLA_DK ** -0.5)) * jnp.exp(b_all)).astype(BF16)
    k_inv = (k_all * jnp.exp(-b_all)).astype(BF16)
    causal = (lax.broadcasted_iota(jnp.int32, (CHUNK, CHUNK), 0)
              >= lax.broadcasted_iota(jnp.int32, (CHUNK, CHUNK), 1))
    nt = (((1,), (1,)), ((), ()))
    units = [(c, h) for c in range(n_chunks) for h in range(GLA_HEADS)]

    def cut(c, h):
        return (slice(c * CHUNK, (c + 1) * CHUNK), slice(h * GLA_DK, (h + 1) * GLA_DK),
                slice(h * GLA_DV, (h + 1) * GLA_DV))

    intra, update, decay = {}, {}, {}
    for c in range(n_chunks):
        rs = slice(c * CHUNK, (c + 1) * CHUNK)
        b_c = b_all[rs]
        b_last = b_c[CHUNK - 1:CHUNK]
        k_dec = (k_all[rs] * jnp.exp(b_last - b_c)).astype(BF16)
        decay_c = jnp.exp(b_last)
        for h in range(GLA_HEADS):
            _, ks, vs = cut(c, h)
            vh = z_ref[0, rs, OFF_V + h * GLA_DV:OFF_V + (h + 1) * GLA_DV]
            scores = lax.dot_general(q_dec[rs, ks], k_inv[rs, ks], nt, preferred_element_type=F32)
            scores = jnp.where(causal, scores, 0.0).astype(BF16)
            intra[c, h] = jnp.dot(scores, vh, preferred_element_type=F32)
            update[c, h] = lax.dot_general(vh, k_dec[:, ks], (((0,), (0,)), ((), ())),
                                           preferred_element_type=F32)
            decay[c, h] = decay_c[:, ks]
    before = {}
    for h in range(GLA_HEADS):
        state = state_ref[h]
        for c in range(n_chunks):
            before[c, h] = state.astype(BF16)
            state = decay[c, h] * state + update[c, h]
        state_ref[h] = state
    for c, h in units:
        rs, ks, vs = cut(c, h)
        o = intra[c, h] + lax.dot_general(q_dec[rs, ks], before[c, h], nt, preferred_element_type=F32)
        mu = jnp.mean(o, axis=-1, keepdims=True)
        oc = o - mu
        var = jnp.mean(oc * oc, axis=-1, keepdims=True)
        o_n = (oc * lax.rsqrt(var + LN_EPS)) * gng_ref[:, vs]
        r_h = z_ref[0, rs, OFF_R + h * GLA_DV:OFF_R + (h + 1) * GLA_DV].astype(F32)
        y_b = (r_h * jax.nn.sigmoid(r_h)) * o_n
        o_ref[0, rs, POOL_WIDTH + h * GLA_DV:POOL_WIDTH + (h + 1) * GLA_DV] = y_b.astype(BF16)


def _even_mix(z, log_a, w_pool, pool_scale, gla_norm_g):
    bsz, seq, zw = z.shape
    rows = min(512, seq)
    hk = GLA_HEADS * GLA_DK
    halo_blocks = rows // HALO
    vmem = (2 * (_nbytes((rows, zw), BF16) + _nbytes((rows, hk), F32) + _nbytes((rows, D_MODEL), BF16)
                 + _nbytes((4, POOL_GROUP, POOL_GROUP), BF16))
            + 12 * _nbytes((rows, POOL_WIDTH), F32) + 4 * 2**20)
    return pl.pallas_call(
        functools.partial(_even_mix_kernel, rows=rows),
        grid=(bsz, seq // rows),
        in_specs=[
            pl.BlockSpec((1, rows, zw), lambda b, t: (b, t, 0)),
            pl.BlockSpec((1, HALO, POOL_WIDTH), lambda b, t: (b, jnp.maximum(t * halo_blocks - 1, 0), 0)),
            pl.BlockSpec((1, rows, hk), lambda b, t: (b, t, 0)),
            pl.BlockSpec((len(POOL_WINDOWS), POOL_GROUP, POOL_GROUP), lambda b, t: (0, 0, 0)),
            pl.BlockSpec((1, POOL_WIDTH), lambda b, t: (0, 0)),
            pl.BlockSpec((1, GLA_HEADS * GLA_DV), lambda b, t: (0, 0)),
        ],
        out_specs=pl.BlockSpec((1, rows, D_MODEL), lambda b, t: (b, t, 0)),
        out_shape=jax.ShapeDtypeStruct((bsz, seq, D_MODEL), BF16),
        scratch_shapes=[pltpu.VMEM((GLA_HEADS, GLA_DV, GLA_DK), F32)],
        compiler_params=_cparams(("arbitrary", "arbitrary"), vmem),
        name="even_mix",
    )(z, z, log_a, w_pool, pool_scale, gla_norm_g)


def _attn_kernel(diag_ref, q_ref, k0_ref, k1_ref, k2_ref, v0_ref, v1_ref, v2_ref, o_ref, bias_ref):
    hb = diag_ref.shape[0]

    @pl.when((pl.program_id(1) == 0) & (pl.program_id(2) == 0))
    def _():
        qpos = lax.broadcasted_iota(jnp.int32, (ATT_Q, ATT_WIN), 0)
        kpos = lax.broadcasted_iota(jnp.int32, (ATT_Q, ATT_WIN), 1)
        lag = kpos // CHUNK - qpos // CHUNK
        in_band = (lag >= 0) & (lag <= ATT_PAST_CHUNKS)
        for h in range(hb):
            rows = jnp.broadcast_to(diag_ref[h], (ATT_Q, ATT_DIAG))
            table = pltpu.roll(rows, 0, 1, stride=1, stride_axis=0)[:, :ATT_WIN] * LOG2_E
            table = jnp.where(in_band, table, NEG_INF)
            for step in range(ATT_KEY_BLOCKS):
                lead = (ATT_KEY_BLOCKS - 1 - step) * ATT_Q
                bias_ref[step, h] = jnp.where(kpos >= lead, table, NEG_INF)

    variant = jnp.minimum(pl.program_id(2), ATT_KEY_BLOCKS - 1)
    k_refs = (k0_ref, k1_ref, k2_ref)
    v_refs = (v0_ref, v1_ref, v2_ref)
    nt = (((1,), (1,)), ((), ()))
    def head_cols(h):
        return slice(h * ATT_HEAD_DIM, (h + 1) * ATT_HEAD_DIM)

    def qk(h):
        q = q_ref[0, :, head_cols(h)]
        return jnp.concatenate(
            [lax.dot_general(q, kr[0, :, head_cols(h)], nt, preferred_element_type=F32) for kr in k_refs],
            axis=1)

    def softmax(h, s):
        s = s + bias_ref[variant, h]
        p = jnp.exp2(s - jnp.max(s, axis=-1, keepdims=True))
        return p.astype(BF16), jnp.sum(p, axis=-1, keepdims=True)

    def pv(h, p, denom):
        o = jnp.dot(p[:, 0:ATT_Q], v_refs[0][0, :, head_cols(h)], preferred_element_type=F32)
        for j in range(1, ATT_KEY_BLOCKS):
            o = o + jnp.dot(p[:, j * ATT_Q:(j + 1) * ATT_Q], v_refs[j][0, :, head_cols(h)],
                            preferred_element_type=F32)
        o_ref[0, :, head_cols(h)] = (o / denom).astype(BF16)

    scores, probs = {}, {}
    for t in range(hb + 2):
        if t < hb:
            scores[t] = qk(t)
        if 0 <= t - 1 < hb:
            probs[t - 1] = softmax(t - 1, scores.pop(t - 1))
        if 0 <= t - 2 < hb:
            pv(t - 2, *probs.pop(t - 2))


def _band_attention(qkv, diag):
    bsz, seq, _ = qkv.shape
    hb = ATT_HB
    width = hb * ATT_HEAD_DIM
    groups = ATT_HEADS // hb
    blk = (1, ATT_Q, width)

    def key_spec(col0, back):
        return pl.BlockSpec(blk, lambda g, b, i: (b, jnp.maximum(i - back, 0), col0 + g))

    vmem = (2 * 8 * _nbytes(blk, BF16) + _nbytes((ATT_KEY_BLOCKS, hb, ATT_Q, ATT_WIN), F32)
            + 8 * _nbytes((ATT_Q, ATT_WIN), F32) + 4 * 2**20)
    return pl.pallas_call(
        _attn_kernel,
        grid=(groups, bsz, seq // ATT_Q),
        in_specs=[
            pl.BlockSpec((hb, 1, ATT_DIAG), lambda g, b, i: (g, 0, 0)),
            pl.BlockSpec(blk, lambda g, b, i: (b, i, g)),
            key_spec(groups, 2), key_spec(groups, 1), key_spec(groups, 0),
            key_spec(2 * groups, 2), key_spec(2 * groups, 1), key_spec(2 * groups, 0),
        ],
        out_specs=pl.BlockSpec(blk, lambda g, b, i: (b, i, g)),
        out_shape=jax.ShapeDtypeStruct((bsz, seq, D_MODEL), BF16),
        scratch_shapes=[pltpu.VMEM((ATT_KEY_BLOCKS, hb, ATT_Q, ATT_WIN), F32)],
        compiler_params=_cparams(("arbitrary", "arbitrary", "arbitrary"), vmem),
        name="band_attention",
    )(diag, qkv, qkv, qkv, qkv, qkv, qkv, qkv)


def _attention_bias_diagonal(rel_bias):
    pad = ATT_PAST_CHUNKS * CHUNK
    m = jnp.arange(ATT_DIAG)
    q_minus_k = jnp.where(m < ATT_WIN, -m, ATT_DIAG - m)
    rel = jnp.clip(pad + q_minus_k, -REL_CLIP, REL_CLIP) + REL_CLIP
    return rel_bias[:, rel].astype(F32)[:, None, :]


def _mm_ln_kernel(a_ref, w_ref, x_ref, gate_ref, lng_ref, lnb_ref, *rest, alpha, emit_u, sub):
    if emit_u:
        sc_ref, sh_ref, xo_ref, uo_ref = rest
    else:
        (xo_ref,) = rest
    for r in range(a_ref.shape[1] // sub):
        rows = slice(r * sub, (r + 1) * sub)
        y = jnp.dot(a_ref[0, rows, :], w_ref[...], preferred_element_type=F32)
        t = alpha * x_ref[0, rows, :] + (1.0 + gate_ref[0]) * y
        mu = jnp.mean(t, axis=-1, keepdims=True)
        tc = t - mu
        var = jnp.mean(tc * tc, axis=-1, keepdims=True)
        xn = (tc * lax.rsqrt(var + LN_EPS)) * lng_ref[...] + lnb_ref[...]
        xo_ref[0, rows, :] = xn
        if emit_u:
            uo_ref[0, rows, :] = (xn * (1.0 + sc_ref[0]) + sh_ref[0]).astype(BF16)


def _mm_ln(a, w, x, gate, ln_g, ln_b, next_sc, next_sh, alpha, bm, name):
    bsz, seq, kdim = a.shape
    d = w.shape[1]
    bm = min(bm, seq)
    sub = min(256, bm)
    emit_u = next_sc is not None
    vec = pl.BlockSpec((1, 1, d), lambda b, i: (b, 0, 0))
    par = pl.BlockSpec((1, d), lambda b, i: (0, 0))
    row = pl.BlockSpec((1, bm, d), lambda b, i: (b, i, 0))
    in_specs = [
        pl.BlockSpec((1, bm, kdim), lambda b, i: (b, i, 0)),
        pl.BlockSpec((kdim, d), lambda b, i: (0, 0), pipeline_mode=pl.Buffered(1)),
        row, vec, par, par,
    ]
    args = [a, w, x, gate, ln_g, ln_b]
    out_specs = [row]
    out_shape = [jax.ShapeDtypeStruct((bsz, seq, d), F32)]
    if emit_u:
        in_specs += [vec, vec]
        args += [next_sc, next_sh]
        out_specs.append(row)
        out_shape.append(jax.ShapeDtypeStruct((bsz, seq, d), BF16))
    vmem = (_nbytes((kdim, d), BF16)
            + 2 * (_nbytes((bm, kdim), BF16) + 2 * _nbytes((bm, d), F32) + _nbytes((bm, d), BF16))
            + 6 * _nbytes((sub, d), F32) + 2 * 2**20)
    outs = pl.pallas_call(
        functools.partial(_mm_ln_kernel, alpha=alpha, emit_u=emit_u, sub=sub),
        grid=(bsz, seq // bm),
        in_specs=in_specs,
        out_specs=out_specs,
        out_shape=out_shape,
        compiler_params=_cparams(("arbitrary", "arbitrary"), vmem),
        name=name,
    )(*args)
    return (outs[0], outs[1]) if emit_u else (outs[0], None)


def _ffn_up_kernel(u_ref, wa_ref, wg_ref, cw_ref, cb_ref, h_ref, tail_ref, *, sub):
    @pl.when(pl.program_id(2) == 0)
    def _():
        tail_ref[...] = jnp.zeros(tail_ref.shape, F32)

    tail = tail_ref[...]
    row = lax.broadcasted_iota(jnp.int32, (sub, 1), 0)
    for r in range(u_ref.shape[1] // sub):
        rows = slice(r * sub, (r + 1) * sub)
        u = u_ref[0, rows, :]
        prev1 = tail[CONV_TAIL - 1:CONV_TAIL]
        prev2 = tail[CONV_TAIL - 2:CONV_TAIL - 1]
        g = jnp.dot(u, wg_ref[...], preferred_element_type=F32)
        g1 = jnp.where(row == 0, prev1, pltpu.roll(g, 1, 0))
        g2 = jnp.where(row == 0, prev2, jnp.where(row == 1, prev1, pltpu.roll(g, 2, 0)))
        gate = jax.nn.gelu(cb_ref[...] + g2 * cw_ref[0:1] + g1 * cw_ref[1:2] + g * cw_ref[2:3])
        a = jnp.dot(u, wa_ref[...], preferred_element_type=F32)
        h_ref[0, rows, :] = (gate * a).astype(BF16)
        tail = g[sub - CONV_TAIL:sub]
    tail_ref[...] = tail


def _ffn_up(u, w_up, conv_w, conv_b):
    bsz, seq, d = u.shape
    bm = min(2048, seq)
    sub = min(1024, bm)
    bn = FFN_TILE
    n_tiles = FFN_PAD // bn
    vmem = (2 * (_nbytes((bm, d), BF16) + 2 * _nbytes((d, bn), BF16) + _nbytes((bm, bn), BF16))
            + 8 * _nbytes((sub, bn), F32) + 2 * 2**20)
    return pl.pallas_call(
        functools.partial(_ffn_up_kernel, sub=sub),
        grid=(bsz, n_tiles, seq // bm),
        in_specs=[
            pl.BlockSpec((1, bm, d), lambda b, j, i: (b, i, 0)),
            pl.BlockSpec((d, bn), lambda b, j, i: (0, j)),
            pl.BlockSpec((d, bn), lambda b, j, i: (0, n_tiles + j)),
            pl.BlockSpec((3, bn), lambda b, j, i: (0, j)),
            pl.BlockSpec((1, bn), lambda b, j, i: (0, j)),
        ],
        out_specs=pl.BlockSpec((1, bm, bn), lambda b, j, i: (b, i, j)),
        out_shape=jax.ShapeDtypeStruct((bsz, seq, FFN_PAD), BF16),
        scratch_shapes=[pltpu.VMEM((CONV_TAIL, bn), F32)],
        compiler_params=_cparams(("arbitrary", "arbitrary", "arbitrary"), vmem),
        name="ffn_up",
    )(u, w_up, w_up, conv_w, conv_b)


def kernel(x, c, w_ada, b_ada, ln_g, ln_b, w_in_ab, w_gate_lr, b_gate, gla_norm_g, w_pool, pool_scale,
           w_out_ab, w_qkv, rel_bias, w_o, w_up, conv_w, conv_b, w_down):
    bsz, seq, d = x.shape
    depth = w_ada.shape[0]
    alpha = (2.0 * depth) ** 0.25
    fpad = FFN_PAD - FFN_DIM

    mod = _ada_modulation(c, w_ada, b_ada).reshape(depth, bsz, 6, 1, d)

    def mod_vec(layer, idx):
        return mod[layer, :, idx]

    u = _modulate(x, mod_vec(0, 1), mod_vec(0, 0))
    for layer in range(depth):
        if layer % 2 == 0:
            e = layer // 2
            w_main = w_in_ab[e, :, :OFF_LR].astype(BF16)
            w_lr = jnp.pad(w_in_ab[e, :, OFF_LR:], ((0, 0), (0, LANES - GLA_GATE_RANK))).astype(BF16)
            w_gate = jnp.pad(w_gate_lr[e], ((0, LANES - GLA_GATE_RANK), (0, 0))).astype(BF16)
            u2d = u.reshape(bsz * seq, d)
            z = _matmul(u2d, w_main, "in_proj").reshape(bsz, seq, OFF_LR)
            log_a = _gla_gate(u2d, w_lr, w_gate, b_gate[e][None]).reshape(bsz, seq, GLA_HEADS * GLA_DK)
            y = _even_mix(z, log_a, w_pool[e].astype(BF16), pool_scale[e][None], gla_norm_g[e][None])
            w_proj = w_out_ab[e].astype(BF16)
        else:
            o = layer // 2
            col_scale = jnp.where(jnp.arange(3 * d) < d, ATT_HEAD_DIM ** -0.5 * LOG2_E, 1.0).astype(F32)
            w_att = (w_qkv[o] * col_scale[None, :]).astype(BF16)
            qkv = _matmul(u.reshape(bsz * seq, d), w_att, "qkv_proj").reshape(bsz, seq, 3 * d)
            y = _band_attention(qkv, _attention_bias_diagonal(rel_bias[o]))
            w_proj = w_o[o].astype(BF16)
        x, u = _mm_ln(y, w_proj, x, mod_vec(layer, 2), ln_g[layer, 0][None], ln_b[layer, 0][None],
                      mod_vec(layer, 4), mod_vec(layer, 3), alpha, 512, "mixer_out_ln")

        w_up_pad = jnp.concatenate([
            jnp.pad(w_up[layer, :, :FFN_DIM], ((0, 0), (0, fpad))),
            jnp.pad(w_up[layer, :, FFN_DIM:], ((0, 0), (0, fpad)))], axis=1).astype(BF16)
        h = _ffn_up(u, w_up_pad, jnp.pad(conv_w[layer], ((0, 0), (0, fpad))),
                    jnp.pad(conv_b[layer], (0, fpad))[None])
        w_down_pad = jnp.pad(w_down[layer], ((0, fpad), (0, 0))).astype(BF16)
        last = layer == depth - 1
        x, u = _mm_ln(h, w_down_pad, x, mod_vec(layer, 5), ln_g[layer, 1][None], ln_b[layer, 1][None],
                      None if last else mod_vec(layer + 1, 1), None if last else mod_vec(layer + 1, 0),
                      alpha, 256, "ffn_down_ln")
    return x
```

```python
import functools

import jax
import jax.numpy as jnp
from jax import lax
from jax.experimental import pallas as pl
from jax.experimental.pallas import tpu as pltpu

F32 = jnp.float32
BF16 = jnp.bfloat16

D_MODEL = 2048
CHUNK = 64
POOL_WIDTH = D_MODEL // 2
POOL_WINDOWS = (2, 4, 8, 16)
POOL_GROUP = POOL_WIDTH // len(POOL_WINDOWS)
GLA_HEADS = 4
GLA_DV = (D_MODEL // 2) // GLA_HEADS
GLA_DK = GLA_DV // 2
GLA_GATE_RANK = 16
GLA_TAU = 16.0
OFF_Q = POOL_WIDTH
OFF_K = OFF_Q + GLA_HEADS * GLA_DK
OFF_V = OFF_K + GLA_HEADS * GLA_DK
OFF_R = OFF_V + GLA_HEADS * GLA_DV
OFF_LR = OFF_R + GLA_HEADS * GLA_DV
ATT_HEADS = 16
ATT_HEAD_DIM = D_MODEL // ATT_HEADS
ATT_PAST_CHUNKS = 8
REL_CLIP = 256
FFN_DIM = 5504
LN_EPS = 1e-5
NEG_INF = -1e30

V7X_VMEM_BYTES = 64 * 2**20
LANES = 128
F32_SUBLANES = 8
BF16_SUBLANES = 16
CONV_TAIL = F32_SUBLANES

FFN_TILE = 512
FFN_PAD = -(-FFN_DIM // FFN_TILE) * FFN_TILE
HALO = BF16_SUBLANES
ATT_Q = 4 * CHUNK
ATT_WIN = ATT_Q + ATT_PAST_CHUNKS * CHUNK
ATT_HB = 8
ATT_KEY_BLOCKS = ATT_WIN // ATT_Q
ATT_DIAG = ATT_WIN + ATT_Q
LOG2_E = 1.4426950408889634


def _cparams(semantics, vmem_bytes):
    limit = min(int(vmem_bytes), V7X_VMEM_BYTES - 4 * 2**20)
    return pltpu.CompilerParams(dimension_semantics=semantics, vmem_limit_bytes=limit)


def _nbytes(shape, dtype):
    n = 1
    for s in shape:
        n *= s
    return n * jnp.dtype(dtype).itemsize


def _ada_kernel(c_ref, w_ref, b_ref, o_ref):
    c = c_ref[...]
    cond = (c * jax.nn.sigmoid(c)).astype(BF16)
    o_ref[0] = jnp.dot(cond, w_ref[0].astype(BF16), preferred_element_type=F32) + b_ref[0]


def _ada_modulation(c, w_ada, b_ada):
    depth, d, n = w_ada.shape
    bsz = c.shape[0]
    rows = -(-bsz // 8) * 8
    bn = 1024
    c_pad = jnp.pad(c, ((0, rows - bsz), (0, 0)))
    vmem = 2 * (_nbytes((d, bn), F32) + _nbytes((rows, bn), F32)) + _nbytes((d, bn), BF16) + 2**20
    out = pl.pallas_call(
        _ada_kernel,
        grid=(depth, n // bn),
        in_specs=[
            pl.BlockSpec((rows, d), lambda l, j: (0, 0)),
            pl.BlockSpec((1, d, bn), lambda l, j: (l, 0, j)),
            pl.BlockSpec((1, 1, bn), lambda l, j: (l, 0, j)),
        ],
        out_specs=pl.BlockSpec((1, rows, bn), lambda l, j: (l, 0, j)),
        out_shape=jax.ShapeDtypeStruct((depth, rows, n), F32),
        compiler_params=_cparams(("arbitrary", "arbitrary"), vmem),
        name="ada_modulation",
    )(c_pad, w_ada, b_ada.reshape(depth, 1, n))
    return out[:, :bsz]


def _modulate_kernel(x_ref, sc_ref, sh_ref, u_ref):
    u_ref[0] = (x_ref[0] * (1.0 + sc_ref[0]) + sh_ref[0]).astype(BF16)


def _modulate(x, sc, sh):
    bsz, seq, d = x.shape
    bs = min(1024, seq)
    vmem = 2 * (_nbytes((bs, d), F32) + _nbytes((bs, d), BF16)) + 2**20
    return pl.pallas_call(
        _modulate_kernel,
        grid=(bsz, seq // bs),
        in_specs=[
            pl.BlockSpec((1, bs, d), lambda b, s: (b, s, 0)),
            pl.BlockSpec((1, 1, d), lambda b, s: (b, 0, 0)),
            pl.BlockSpec((1, 1, d), lambda b, s: (b, 0, 0)),
        ],
        out_specs=pl.BlockSpec((1, bs, d), lambda b, s: (b, s, 0)),
        out_shape=jax.ShapeDtypeStruct((bsz, seq, d), BF16),
        compiler_params=_cparams(("arbitrary", "arbitrary"), vmem),
        name="modulate",
    )(x, sc, sh)


def _matmul_kernel(a_ref, w_ref, o_ref):
    o_ref[...] = jnp.dot(a_ref[...], w_ref[...], preferred_element_type=F32).astype(o_ref.dtype)


def _matmul(a, w, name):
    m, k = a.shape
    n = w.shape[1]
    bm = min(1024, m)
    bn = min(2048, n)
    vmem = (2 * (_nbytes((bm, k), BF16) + _nbytes((k, bn), BF16) + _nbytes((bm, bn), BF16))
            + _nbytes((bm, bn), F32) + 2**20)
    return pl.pallas_call(
        _matmul_kernel,
        grid=(m // bm, n // bn),
        in_specs=[
            pl.BlockSpec((bm, k), lambda i, j: (i, 0)),
            pl.BlockSpec((k, bn), lambda i, j: (0, j)),
        ],
        out_specs=pl.BlockSpec((bm, bn), lambda i, j: (i, j)),
        out_shape=jax.ShapeDtypeStruct((m, n), BF16),
        compiler_params=_cparams(("arbitrary", "arbitrary"), vmem),
        name=name,
    )(a, w)


def _in_proj_kernel(u_ref, w_ref, wlr_ref, wg_ref, bg_ref, z_ref, la_ref):
    z_ref[...] = jnp.dot(u_ref[...], w_ref[...], preferred_element_type=F32).astype(BF16)

    @pl.when(pl.program_id(1) == 0)
    def _():
        z_lr = jnp.dot(u_ref[...], wlr_ref[...], preferred_element_type=F32).astype(BF16)
        pre = jnp.dot(z_lr, wg_ref[...], preferred_element_type=F32) + bg_ref[...]
        log_sig = jnp.minimum(pre, 0.0) - jnp.log1p(jnp.exp(-jnp.abs(pre)))
        la_ref[...] = log_sig * (1.0 / GLA_TAU)


def _in_proj(u2d, w_main, w_lr, w_gate, b_gate):
    m, k = u2d.shape
    n = w_main.shape[1]
    ng = w_gate.shape[1]
    bm = min(1024, m)
    bn = min(2048, n)
    vmem = (2 * (_nbytes((bm, k), BF16) + _nbytes((k, bn), BF16) + _nbytes((bm, bn), BF16)
                 + _nbytes((k, LANES), BF16) + _nbytes((bm, ng), F32))
            + _nbytes((bm, bn), F32) + 4 * _nbytes((bm, ng), F32) + 2**20)
    return pl.pallas_call(
        _in_proj_kernel,
        grid=(m // bm, n // bn),
        in_specs=[
            pl.BlockSpec((bm, k), lambda i, j: (i, 0)),
            pl.BlockSpec((k, bn), lambda i, j: (0, j)),
            pl.BlockSpec((k, LANES), lambda i, j: (0, 0)),
            pl.BlockSpec((LANES, ng), lambda i, j: (0, 0)),
            pl.BlockSpec((1, ng), lambda i, j: (0, 0)),
        ],
        out_specs=[pl.BlockSpec((bm, bn), lambda i, j: (i, j)),
                   pl.BlockSpec((bm, ng), lambda i, j: (i, 0))],
        out_shape=[jax.ShapeDtypeStruct((m, n), BF16), jax.ShapeDtypeStruct((m, ng), F32)],
        compiler_params=_cparams(("arbitrary", "arbitrary"), vmem),
        name="in_proj",
    )(u2d, w_main, w_lr, w_gate, b_gate)


def _even_mix_kernel(z_ref, halo_ref, la_ref, wpool_ref, pscale_ref, gng_ref, o_ref, state_ref, *, rows):
    t = pl.program_id(1)
    first = t == 0
    n_chunks = rows // CHUNK

    halo = jnp.where(first, 0.0, halo_ref[0].astype(F32))
    ext = jnp.concatenate([halo, z_ref[0, :, 0:POOL_WIDTH].astype(F32)], axis=0)
    grow = lax.broadcasted_iota(jnp.int32, (rows, 1), 0) + t * rows
    for gi, w in enumerate(POOL_WINDOWS):
        cols = slice(gi * POOL_GROUP, (gi + 1) * POOL_GROUP)
        e = ext[:, cols]
        s = e
        step = 1
        while step < w:
            s = s + pltpu.roll(s, step, 0)
            step *= 2
        count = jnp.minimum(grow + 1, w).astype(F32)
        dev = s[HALO:] / count - e[HALO:]
        y_a = jnp.dot(dev.astype(BF16), wpool_ref[gi], preferred_element_type=F32) * pscale_ref[:, cols]
        o_ref[0, :, cols] = y_a.astype(BF16)

    @pl.when(first)
    def _():
        state_ref[...] = jnp.zeros_like(state_ref)

    in_chunk = lax.broadcasted_iota(jnp.int32, (rows, 1), 0) % CHUNK
    b_all = la_ref[0]
    step = 1
    while step < CHUNK:
        b_all = b_all + jnp.where(in_chunk >= step, pltpu.roll(b_all, step, 0), 0.0)
        step *= 2
    q_all = z_ref[0, :, OFF_Q:OFF_K].astype(F32)
    k_all = z_ref[0, :, OFF_K:OFF_V].astype(F32)
    q_dec = ((q_all * (GLA_DK ** -0.5)) * jnp.exp(b_all)).astype(BF16)
    k_inv = (k_all * jnp.exp(-b_all)).astype(BF16)
    causal = (lax.broadcasted_iota(jnp.int32, (CHUNK, CHUNK), 0)
              >= lax.broadcasted_iota(jnp.int32, (CHUNK, CHUNK), 1))
    nt = (((1,), (1,)), ((), ()))
    units = [(c, h) for c in range(n_chunks) for h in range(GLA_HEADS)]

    def cut(c, h):
        return (slice(c * CHUNK, (c + 1) * CHUNK), slice(h * GLA_DK, (h + 1) * GLA_DK),
                slice(h * GLA_DV, (h + 1) * GLA_DV))

    intra, update, decay = {}, {}, {}
    for c in range(n_chunks):
        rs = slice(c * CHUNK, (c + 1) * CHUNK)
        b_c = b_all[rs]
        b_last = b_c[CHUNK - 1:CHUNK]
        k_dec = (k_all[rs] * jnp.exp(b_last - b_c)).astype(BF16)
        decay_c = jnp.exp(b_last)
        for h in range(GLA_HEADS):
            _, ks, vs = cut(c, h)
            vh = z_ref[0, rs, OFF_V + h * GLA_DV:OFF_V + (h + 1) * GLA_DV]
            scores = lax.dot_general(q_dec[rs, ks], k_inv[rs, ks], nt, preferred_element_type=F32)
            scores = jnp.where(causal, scores, 0.0).astype(BF16)
            intra[c, h] = jnp.dot(scores, vh, preferred_element_type=F32)
            update[c, h] = lax.dot_general(vh, k_dec[:, ks], (((0,), (0,)), ((), ())),
                                           preferred_element_type=F32)
            decay[c, h] = decay_c[:, ks]
    before = {}
    for h in range(GLA_HEADS):
        state = state_ref[h]
        for c in range(n_chunks):
            before[c, h] = state.astype(BF16)
            state = decay[c, h] * state + update[c, h]
        state_ref[h] = state
    for c, h in units:
        rs, ks, vs = cut(c, h)
        o = intra[c, h] + lax.dot_general(q_dec[rs, ks], before[c, h], nt, preferred_element_type=F32)
        mu = jnp.mean(o, axis=-1, keepdims=True)
        oc = o - mu
        var = jnp.mean(oc * oc, axis=-1, keepdims=True)
        o_n = (oc * lax.rsqrt(var + LN_EPS)) * gng_ref[:, vs]
        r_h = z_ref[0, rs, OFF_R + h * GLA_DV:OFF_R + (h + 1) * GLA_DV].astype(F32)
        y_b = (r_h * jax.nn.sigmoid(r_h)) * o_n
        o_ref[0, rs, POOL_WIDTH + h * GLA_DV:POOL_WIDTH + (h + 1) * GLA_DV] = y_b.astype(BF16)


def _even_mix(z, log_a, w_pool, pool_scale, gla_norm_g):
    bsz, seq, zw = z.shape
    rows = min(512, seq)
    hk = GLA_HEADS * GLA_DK
    halo_blocks = rows // HALO
    vmem = (2 * (_nbytes((rows, zw), BF16) + _nbytes((rows, hk), F32) + _nbytes((rows, D_MODEL), BF16)
                 + _nbytes((4, POOL_GROUP, POOL_GROUP), BF16))
            + 12 * _nbytes((rows, POOL_WIDTH), F32) + 4 * 2**20)
    return pl.pallas_call(
        functools.partial(_even_mix_kernel, rows=rows),
        grid=(bsz, seq // rows),
        in_specs=[
            pl.BlockSpec((1, rows, zw), lambda b, t: (b, t, 0)),
            pl.BlockSpec((1, HALO, POOL_WIDTH), lambda b, t: (b, jnp.maximum(t * halo_blocks - 1, 0), 0)),
            pl.BlockSpec((1, rows, hk), lambda b, t: (b, t, 0)),
            pl.BlockSpec((len(POOL_WINDOWS), POOL_GROUP, POOL_GROUP), lambda b, t: (0, 0, 0)),
            pl.BlockSpec((1, POOL_WIDTH), lambda b, t: (0, 0)),
            pl.BlockSpec((1, GLA_HEADS * GLA_DV), lambda b, t: (0, 0)),
        ],
        out_specs=pl.BlockSpec((1, rows, D_MODEL), lambda b, t: (b, t, 0)),
        out_shape=jax.ShapeDtypeStruct((bsz, seq, D_MODEL), BF16),
        scratch_shapes=[pltpu.VMEM((GLA_HEADS, GLA_DV, GLA_DK), F32)],
        compiler_params=_cparams(("arbitrary", "arbitrary"), vmem),
        name="even_mix",
    )(z, z, log_a, w_pool, pool_scale, gla_norm_g)


def _attn_kernel(diag_ref, q_ref, k0_ref, k1_ref, k2_ref, v0_ref, v1_ref, v2_ref, o_ref, bias_ref):
    hb = diag_ref.shape[0]

    @pl.when((pl.program_id(1) == 0) & (pl.program_id(2) == 0))
    def _():
        qpos = lax.broadcasted_iota(jnp.int32, (ATT_Q, ATT_WIN), 0)
        kpos = lax.broadcasted_iota(jnp.int32, (ATT_Q, ATT_WIN), 1)
        lag = kpos // CHUNK - qpos // CHUNK
        in_band = (lag >= 0) & (lag <= ATT_PAST_CHUNKS)
        for h in range(hb):
            rows = jnp.broadcast_to(diag_ref[h], (ATT_Q, ATT_DIAG))
            table = pltpu.roll(rows, 0, 1, stride=1, stride_axis=0)[:, :ATT_WIN] * LOG2_E
            table = jnp.where(in_band, table, NEG_INF)
            for step in range(ATT_KEY_BLOCKS):
                lead = (ATT_KEY_BLOCKS - 1 - step) * ATT_Q
                bias_ref[step, h] = jnp.where(kpos >= lead, table, NEG_INF)

    variant = jnp.minimum(pl.program_id(2), ATT_KEY_BLOCKS - 1)
    k_refs = (k0_ref, k1_ref, k2_ref)
    v_refs = (v0_ref, v1_ref, v2_ref)
    nt = (((1,), (1,)), ((), ()))
    def head_cols(h):
        return slice(h * ATT_HEAD_DIM, (h + 1) * ATT_HEAD_DIM)

    def qk(h):
        q = q_ref[0, :, head_cols(h)]
        return jnp.concatenate(
            [lax.dot_general(q, kr[0, :, head_cols(h)], nt, preferred_element_type=F32) for kr in k_refs],
            axis=1)

    def softmax(h, s):
        s = s + bias_ref[variant, h]
        p = jnp.exp2(s - jnp.max(s, axis=-1, keepdims=True))
        return p.astype(BF16), jnp.sum(p, axis=-1, keepdims=True)

    def pv(h, p, denom):
        o = jnp.dot(p[:, 0:ATT_Q], v_refs[0][0, :, head_cols(h)], preferred_element_type=F32)
        for j in range(1, ATT_KEY_BLOCKS):
            o = o + jnp.dot(p[:, j * ATT_Q:(j + 1) * ATT_Q], v_refs[j][0, :, head_cols(h)],
                            preferred_element_type=F32)
        o_ref[0, :, head_cols(h)] = (o / denom).astype(BF16)

    scores, probs = {}, {}
    for t in range(hb + 2):
        if t < hb:
            scores[t] = qk(t)
        if 0 <= t - 1 < hb:
            probs[t - 1] = softmax(t - 1, scores.pop(t - 1))
        if 0 <= t - 2 < hb:
            pv(t - 2, *probs.pop(t - 2))


def _band_attention(qkv, diag):
    bsz, seq, _ = qkv.shape
    hb = ATT_HB
    width = hb * ATT_HEAD_DIM
    groups = ATT_HEADS // hb
    blk = (1, ATT_Q, width)

    def key_spec(col0, back):
        return pl.BlockSpec(blk, lambda g, b, i: (b, jnp.maximum(i - back, 0), col0 + g))

    vmem = (2 * 8 * _nbytes(blk, BF16) + _nbytes((ATT_KEY_BLOCKS, hb, ATT_Q, ATT_WIN), F32)
            + 8 * _nbytes((ATT_Q, ATT_WIN), F32) + 4 * 2**20)
    return pl.pallas_call(
        _attn_kernel,
        grid=(groups, bsz, seq // ATT_Q),
        in_specs=[
            pl.BlockSpec((hb, 1, ATT_DIAG), lambda g, b, i: (g, 0, 0)),
            pl.BlockSpec(blk, lambda g, b, i: (b, i, g)),
            key_spec(groups, 2), key_spec(groups, 1), key_spec(groups, 0),
            key_spec(2 * groups, 2), key_spec(2 * groups, 1), key_spec(2 * groups, 0),
        ],
        out_specs=pl.BlockSpec(blk, lambda g, b, i: (b, i, g)),
        out_shape=jax.ShapeDtypeStruct((bsz, seq, D_MODEL), BF16),
        scratch_shapes=[pltpu.VMEM((ATT_KEY_BLOCKS, hb, ATT_Q, ATT_WIN), F32)],
        compiler_params=_cparams(("arbitrary", "arbitrary", "arbitrary"), vmem),
        name="band_attention",
    )(diag, qkv, qkv, qkv, qkv, qkv, qkv, qkv)


def _attention_bias_diagonal(rel_bias):
    pad = ATT_PAST_CHUNKS * CHUNK
    m = jnp.arange(ATT_DIAG)
    q_minus_k = jnp.where(m < ATT_WIN, -m, ATT_DIAG - m)
    rel = jnp.clip(pad + q_minus_k, -REL_CLIP, REL_CLIP) + REL_CLIP
    return rel_bias[:, rel].astype(F32)[:, None, :]


def _mm_ln_kernel(a_ref, w_ref, x_ref, gate_ref, lng_ref, lnb_ref, *rest, alpha, emit_u, sub):
    if emit_u:
        sc_ref, sh_ref, xo_ref, uo_ref = rest
    else:
        (xo_ref,) = rest
    for r in range(a_ref.shape[1] // sub):
        rows = slice(r * sub, (r + 1) * sub)
        y = jnp.dot(a_ref[0, rows, :], w_ref[...], preferred_element_type=F32)
        t = alpha * x_ref[0, rows, :] + (1.0 + gate_ref[0]) * y
        mu = jnp.mean(t, axis=-1, keepdims=True)
        tc = t - mu
        var = jnp.mean(tc * tc, axis=-1, keepdims=True)
        xn = (tc * lax.rsqrt(var + LN_EPS)) * lng_ref[...] + lnb_ref[...]
        xo_ref[0, rows, :] = xn
        if emit_u:
            uo_ref[0, rows, :] = (xn * (1.0 + sc_ref[0]) + sh_ref[0]).astype(BF16)


def _mm_ln(a, w, x, gate, ln_g, ln_b, next_sc, next_sh, alpha, bm, name):
    bsz, seq, kdim = a.shape
    d = w.shape[1]
    bm = min(bm, seq)
    sub = min(256, bm)
    emit_u = next_sc is not None
    vec = pl.BlockSpec((1, 1, d), lambda b, i: (b, 0, 0))
    par = pl.BlockSpec((1, d), lambda b, i: (0, 0))
    row = pl.BlockSpec((1, bm, d), lambda b, i: (b, i, 0))
    in_specs = [
        pl.BlockSpec((1, bm, kdim), lambda b, i: (b, i, 0)),
        pl.BlockSpec((kdim, d), lambda b, i: (0, 0), pipeline_mode=pl.Buffered(1)),
        row, vec, par, par,
    ]
    args = [a, w, x, gate, ln_g, ln_b]
    out_specs = [row]
    out_shape = [jax.ShapeDtypeStruct((bsz, seq, d), F32)]
    if emit_u:
        in_specs += [vec, vec]
        args += [next_sc, next_sh]
        out_specs.append(row)
        out_shape.append(jax.ShapeDtypeStruct((bsz, seq, d), BF16))
    vmem = (_nbytes((kdim, d), BF16)
            + 2 * (_nbytes((bm, kdim), BF16) + 2 * _nbytes((bm, d), F32) + _nbytes((bm, d), BF16))
            + 6 * _nbytes((sub, d), F32) + 2 * 2**20)
    outs = pl.pallas_call(
        functools.partial(_mm_ln_kernel, alpha=alpha, emit_u=emit_u, sub=sub),
        grid=(bsz, seq // bm),
        in_specs=in_specs,
        out_specs=out_specs,
        out_shape=out_shape,
        compiler_params=_cparams(("arbitrary", "arbitrary"), vmem),
        name=name,
    )(*args)
    return (outs[0], outs[1]) if emit_u else (outs[0], None)


def _ffn_up_kernel(u_ref, wa_ref, wg_ref, cw_ref, cb_ref, h_ref, tail_ref, *, sub):
    @pl.when(pl.program_id(2) == 0)
    def _():
        tail_ref[...] = jnp.zeros(tail_ref.shape, F32)

    tail = tail_ref[...]
    row = lax.broadcasted_iota(jnp.int32, (sub, 1), 0)
    for r in range(u_ref.shape[1] // sub):
        rows = slice(r * sub, (r + 1) * sub)
        u = u_ref[0, rows, :]
        prev1 = tail[CONV_TAIL - 1:CONV_TAIL]
        prev2 = tail[CONV_TAIL - 2:CONV_TAIL - 1]
        g = jnp.dot(u, wg_ref[...], preferred_element_type=F32)
        g1 = jnp.where(row == 0, prev1, pltpu.roll(g, 1, 0))
        g2 = jnp.where(row == 0, prev2, jnp.where(row == 1, prev1, pltpu.roll(g, 2, 0)))
        gate = jax.nn.gelu(cb_ref[...] + g2 * cw_ref[0:1] + g1 * cw_ref[1:2] + g * cw_ref[2:3])
        a = jnp.dot(u, wa_ref[...], preferred_element_type=F32)
        h_ref[0, rows, :] = (gate * a).astype(BF16)
        tail = g[sub - CONV_TAIL:sub]
    tail_ref[...] = tail


def _ffn_up(u, w_up, conv_w, conv_b):
    bsz, seq, d = u.shape
    bm = min(2048, seq)
    sub = min(1024, bm)
    bn = FFN_TILE
    n_tiles = FFN_PAD // bn
    vmem = (2 * (_nbytes((bm, d), BF16) + 2 * _nbytes((d, bn), BF16) + _nbytes((bm, bn), BF16))
            + 8 * _nbytes((sub, bn), F32) + 2 * 2**20)
    return pl.pallas_call(
        functools.partial(_ffn_up_kernel, sub=sub),
        grid=(bsz, n_tiles, seq // bm),
        in_specs=[
            pl.BlockSpec((1, bm, d), lambda b, j, i: (b, i, 0)),
            pl.BlockSpec((d, bn), lambda b, j, i: (0, j)),
            pl.BlockSpec((d, bn), lambda b, j, i: (0, n_tiles + j)),
            pl.BlockSpec((3, bn), lambda b, j, i: (0, j)),
            pl.BlockSpec((1, bn), lambda b, j, i: (0, j)),
        ],
        out_specs=pl.BlockSpec((1, bm, bn), lambda b, j, i: (b, i, j)),
        out_shape=jax.ShapeDtypeStruct((bsz, seq, FFN_PAD), BF16),
        scratch_shapes=[pltpu.VMEM((CONV_TAIL, bn), F32)],
        compiler_params=_cparams(("arbitrary", "arbitrary", "arbitrary"), vmem),
        name="ffn_up",
    )(u, w_up, w_up, conv_w, conv_b)


def kernel(x, c, w_ada, b_ada, ln_g, ln_b, w_in_ab, w_gate_lr, b_gate, gla_norm_g, w_pool, pool_scale,
           w_out_ab, w_qkv, rel_bias, w_o, w_up, conv_w, conv_b, w_down):
    bsz, seq, d = x.shape
    depth = w_ada.shape[0]
    alpha = (2.0 * depth) ** 0.25
    fpad = FFN_PAD - FFN_DIM

    mod = _ada_modulation(c, w_ada, b_ada).reshape(depth, bsz, 6, 1, d)

    def mod_vec(layer, idx):
        return mod[layer, :, idx]

    u = _modulate(x, mod_vec(0, 1), mod_vec(0, 0))
    for layer in range(depth):
        if layer % 2 == 0:
            e = layer // 2
            w_main = w_in_ab[e, :, :OFF_LR].astype(BF16)
            w_lr = jnp.pad(w_in_ab[e, :, OFF_LR:], ((0, 0), (0, LANES - GLA_GATE_RANK))).astype(BF16)
            w_gate = jnp.pad(w_gate_lr[e], ((0, LANES - GLA_GATE_RANK), (0, 0))).astype(BF16)
            u2d = u.reshape(bsz * seq, d)
            z, log_a = _in_proj(u2d, w_main, w_lr, w_gate, b_gate[e][None])
            y = _even_mix(z.reshape(bsz, seq, OFF_LR), log_a.reshape(bsz, seq, GLA_HEADS * GLA_DK),
                          w_pool[e].astype(BF16), pool_scale[e][None], gla_norm_g[e][None])
            w_proj = w_out_ab[e].astype(BF16)
        else:
            o = layer // 2
            col_scale = jnp.where(jnp.arange(3 * d) < d, ATT_HEAD_DIM ** -0.5 * LOG2_E, 1.0).astype(F32)
            w_att = (w_qkv[o] * col_scale[None, :]).astype(BF16)
            qkv = _matmul(u.reshape(bsz * seq, d), w_att, "qkv_proj").reshape(bsz, seq, 3 * d)
            y = _band_attention(qkv, _attention_bias_diagonal(rel_bias[o]))
            w_proj = w_o[o].astype(BF16)
        x, u = _mm_ln(y, w_proj, x, mod_vec(layer, 2), ln_g[layer, 0][None], ln_b[layer, 0][None],
                      mod_vec(layer, 4), mod_vec(layer, 3), alpha, 512, "mixer_out_ln")

        w_up_pad = jnp.concatenate([
            jnp.pad(w_up[layer, :, :FFN_DIM], ((0, 0), (0, fpad))),
            jnp.pad(w_up[layer, :, FFN_DIM:], ((0, 0), (0, fpad)))], axis=1).astype(BF16)
        h = _ffn_up(u, w_up_pad, jnp.pad(conv_w[layer], ((0, 0), (0, fpad))),
                    jnp.pad(conv_b[layer], (0, fpad))[None])
        w_down_pad = jnp.pad(w_down[layer], ((0, fpad), (0, 0))).astype(BF16)
        last = layer == depth - 1
        x, u = _mm_ln(h, w_down_pad, x, mod_vec(layer, 5), ln_g[layer, 1][None], ln_b[layer, 1][None],
                      None if last else mod_vec(layer + 1, 1), None if last else mod_vec(layer + 1, 0),
                      alpha, 512, "ffn_down_ln")
    return x
```

```python
import functools

import jax
import jax.numpy as jnp
from jax import lax
from jax.experimental import pallas as pl
from jax.experimental.pallas import tpu as pltpu

F32 = jnp.float32
BF16 = jnp.bfloat16

D_MODEL = 2048
CHUNK = 64
POOL_WIDTH = D_MODEL // 2
POOL_WINDOWS = (2, 4, 8, 16)
POOL_GROUP = POOL_WIDTH // len(POOL_WINDOWS)
GLA_HEADS = 4
GLA_DV = (D_MODEL // 2) // GLA_HEADS
GLA_DK = GLA_DV // 2
GLA_GATE_RANK = 16
GLA_TAU = 16.0
OFF_Q = POOL_WIDTH
OFF_K = OFF_Q + GLA_HEADS * GLA_DK
OFF_V = OFF_K + GLA_HEADS * GLA_DK
OFF_R = OFF_V + GLA_HEADS * GLA_DV
OFF_LR = OFF_R + GLA_HEADS * GLA_DV
ATT_HEADS = 16
ATT_HEAD_DIM = D_MODEL // ATT_HEADS
ATT_PAST_CHUNKS = 8
REL_CLIP = 256
FFN_DIM = 5504
LN_EPS = 1e-5
NEG_INF = -1e30

V7X_VMEM_BYTES = 64 * 2**20
LANES = 128
F32_SUBLANES = 8
BF16_SUBLANES = 16
CONV_TAIL = F32_SUBLANES

FFN_TILE = 512
FFN_PAD = -(-FFN_DIM // FFN_TILE) * FFN_TILE
HALO = BF16_SUBLANES
ATT_Q = 4 * CHUNK
ATT_WIN = ATT_Q + ATT_PAST_CHUNKS * CHUNK
ATT_HB = 8
ATT_KEY_BLOCKS = ATT_WIN // ATT_Q
ATT_DIAG = ATT_WIN + ATT_Q
LOG2_E = 1.4426950408889634


def _cparams(semantics, vmem_bytes):
    limit = min(int(vmem_bytes), V7X_VMEM_BYTES - 4 * 2**20)
    return pltpu.CompilerParams(dimension_semantics=semantics, vmem_limit_bytes=limit)


def _nbytes(shape, dtype):
    n = 1
    for s in shape:
        n *= s
    return n * jnp.dtype(dtype).itemsize


def _ada_kernel(c_ref, w_ref, b_ref, o_ref):
    c = c_ref[...]
    cond = (c * jax.nn.sigmoid(c)).astype(BF16)
    o_ref[0] = jnp.dot(cond, w_ref[0].astype(BF16), preferred_element_type=F32) + b_ref[0]


def _ada_modulation(c, w_ada, b_ada):
    depth, d, n = w_ada.shape
    bsz = c.shape[0]
    rows = -(-bsz // 8) * 8
    bn = 1024
    c_pad = jnp.pad(c, ((0, rows - bsz), (0, 0)))
    vmem = 2 * (_nbytes((d, bn), F32) + _nbytes((rows, bn), F32)) + _nbytes((d, bn), BF16) + 2**20
    out = pl.pallas_call(
        _ada_kernel,
        grid=(depth, n // bn),
        in_specs=[
            pl.BlockSpec((rows, d), lambda l, j: (0, 0)),
            pl.BlockSpec((1, d, bn), lambda l, j: (l, 0, j)),
            pl.BlockSpec((1, 1, bn), lambda l, j: (l, 0, j)),
        ],
        out_specs=pl.BlockSpec((1, rows, bn), lambda l, j: (l, 0, j)),
        out_shape=jax.ShapeDtypeStruct((depth, rows, n), F32),
        compiler_params=_cparams(("arbitrary", "arbitrary"), vmem),
        name="ada_modulation",
    )(c_pad, w_ada, b_ada.reshape(depth, 1, n))
    return out[:, :bsz]


def _modulate_kernel(x_ref, sc_ref, sh_ref, u_ref):
    u_ref[0] = (x_ref[0] * (1.0 + sc_ref[0]) + sh_ref[0]).astype(BF16)


def _modulate(x, sc, sh):
    bsz, seq, d = x.shape
    bs = min(1024, seq)
    vmem = 2 * (_nbytes((bs, d), F32) + _nbytes((bs, d), BF16)) + 2**20
    return pl.pallas_call(
        _modulate_kernel,
        grid=(bsz, seq // bs),
        in_specs=[
            pl.BlockSpec((1, bs, d), lambda b, s: (b, s, 0)),
            pl.BlockSpec((1, 1, d), lambda b, s: (b, 0, 0)),
            pl.BlockSpec((1, 1, d), lambda b, s: (b, 0, 0)),
        ],
        out_specs=pl.BlockSpec((1, bs, d), lambda b, s: (b, s, 0)),
        out_shape=jax.ShapeDtypeStruct((bsz, seq, d), BF16),
        compiler_params=_cparams(("arbitrary", "arbitrary"), vmem),
        name="modulate",
    )(x, sc, sh)


def _matmul_kernel(a_ref, w_ref, o_ref):
    o_ref[...] = jnp.dot(a_ref[...], w_ref[...], preferred_element_type=F32).astype(o_ref.dtype)


def _matmul(a, w, name):
    m, k = a.shape
    n = w.shape[1]
    bm = min(1024, m)
    bn = min(2048, n)
    vmem = (2 * (_nbytes((bm, k), BF16) + _nbytes((k, bn), BF16) + _nbytes((bm, bn), BF16))
            + _nbytes((bm, bn), F32) + 2**20)
    return pl.pallas_call(
        _matmul_kernel,
        grid=(m // bm, n // bn),
        in_specs=[
            pl.BlockSpec((bm, k), lambda i, j: (i, 0)),
            pl.BlockSpec((k, bn), lambda i, j: (0, j)),
        ],
        out_specs=pl.BlockSpec((bm, bn), lambda i, j: (i, j)),
        out_shape=jax.ShapeDtypeStruct((m, n), BF16),
        compiler_params=_cparams(("arbitrary", "arbitrary"), vmem),
        name=name,
    )(a, w)


def _mix_rows(z, log_a, halo, states, first_row, wpool_ref, pscale_ref, gng_ref, o_ref, out_row):
    rows = z.shape[0]
    n_chunks = rows // CHUNK

    pooled = z[:, 0:POOL_WIDTH].astype(F32)
    ext = jnp.concatenate([halo, pooled], axis=0)
    grow = lax.broadcasted_iota(jnp.int32, (rows, 1), 0) + first_row
    for gi, w in enumerate(POOL_WINDOWS):
        cols = slice(gi * POOL_GROUP, (gi + 1) * POOL_GROUP)
        e = ext[:, cols]
        s = e
        step = 1
        while step < w:
            s = s + pltpu.roll(s, step, 0)
            step *= 2
        count = jnp.minimum(grow + 1, w).astype(F32)
        dev = s[HALO:] / count - e[HALO:]
        y_a = jnp.dot(dev.astype(BF16), wpool_ref[gi], preferred_element_type=F32) * pscale_ref[:, cols]
        o_ref[0, out_row:out_row + rows, cols] = y_a.astype(BF16)

    in_chunk = lax.broadcasted_iota(jnp.int32, (rows, 1), 0) % CHUNK
    b_all = log_a
    step = 1
    while step < CHUNK:
        b_all = b_all + jnp.where(in_chunk >= step, pltpu.roll(b_all, step, 0), 0.0)
        step *= 2
    q_all = z[:, OFF_Q:OFF_K].astype(F32)
    k_all = z[:, OFF_K:OFF_V].astype(F32)
    q_dec = ((q_all * (GLA_DK ** -0.5)) * jnp.exp(b_all)).astype(BF16)
    k_inv = (k_all * jnp.exp(-b_all)).astype(BF16)
    causal = (lax.broadcasted_iota(jnp.int32, (CHUNK, CHUNK), 0)
              >= lax.broadcasted_iota(jnp.int32, (CHUNK, CHUNK), 1))
    nt = (((1,), (1,)), ((), ()))
    units = [(c, h) for c in range(n_chunks) for h in range(GLA_HEADS)]

    def cut(c, h):
        return (slice(c * CHUNK, (c + 1) * CHUNK), slice(h * GLA_DK, (h + 1) * GLA_DK),
                slice(h * GLA_DV, (h + 1) * GLA_DV))

    intra, update, decay = {}, {}, {}
    for c in range(n_chunks):
        rs = slice(c * CHUNK, (c + 1) * CHUNK)
        b_c = b_all[rs]
        b_last = b_c[CHUNK - 1:CHUNK]
        k_dec = (k_all[rs] * jnp.exp(b_last - b_c)).astype(BF16)
        decay_c = jnp.exp(b_last)
        for h in range(GLA_HEADS):
            _, ks, vs = cut(c, h)
            vh = z[rs, OFF_V + h * GLA_DV:OFF_V + (h + 1) * GLA_DV]
            scores = lax.dot_general(q_dec[rs, ks], k_inv[rs, ks], nt, preferred_element_type=F32)
            scores = jnp.where(causal, scores, 0.0).astype(BF16)
            intra[c, h] = jnp.dot(scores, vh, preferred_element_type=F32)
            update[c, h] = lax.dot_general(vh, k_dec[:, ks], (((0,), (0,)), ((), ())),
                                           preferred_element_type=F32)
            decay[c, h] = decay_c[:, ks]
    before = {}
    new_states = []
    for h in range(GLA_HEADS):
        state = states[h]
        for c in range(n_chunks):
            before[c, h] = state.astype(BF16)
            state = decay[c, h] * state + update[c, h]
        new_states.append(state)
    for c, h in units:
        rs, ks, vs = cut(c, h)
        o = intra[c, h] + lax.dot_general(q_dec[rs, ks], before[c, h], nt, preferred_element_type=F32)
        mu = jnp.mean(o, axis=-1, keepdims=True)
        oc = o - mu
        var = jnp.mean(oc * oc, axis=-1, keepdims=True)
        o_n = (oc * lax.rsqrt(var + LN_EPS)) * gng_ref[:, vs]
        r_h = z[rs, OFF_R + h * GLA_DV:OFF_R + (h + 1) * GLA_DV].astype(F32)
        y_b = (r_h * jax.nn.sigmoid(r_h)) * o_n
        o_ref[0, out_row + c * CHUNK:out_row + (c + 1) * CHUNK,
              POOL_WIDTH + h * GLA_DV:POOL_WIDTH + (h + 1) * GLA_DV] = y_b.astype(BF16)
    return pooled[rows - HALO:rows], new_states


def _even_front_kernel(u_ref, w_ref, wlr_ref, wg_ref, bg_ref, wpool_ref, pscale_ref, gng_ref, o_ref,
                       state_ref, halo_ref, *, sub):
    t = pl.program_id(1)
    rows = u_ref.shape[1]

    @pl.when(t == 0)
    def _():
        state_ref[...] = jnp.zeros(state_ref.shape, F32)
        halo_ref[...] = jnp.zeros(halo_ref.shape, F32)

    states = [state_ref[h] for h in range(GLA_HEADS)]
    halo = halo_ref[...]
    for r in range(rows // sub):
        u = u_ref[0, r * sub:(r + 1) * sub, :]
        z = jnp.dot(u, w_ref[...], preferred_element_type=F32).astype(BF16)
        z_lr = jnp.dot(u, wlr_ref[...], preferred_element_type=F32).astype(BF16)
        pre = jnp.dot(z_lr, wg_ref[...], preferred_element_type=F32) + bg_ref[...]
        log_a = (jnp.minimum(pre, 0.0) - jnp.log1p(jnp.exp(-jnp.abs(pre)))) * (1.0 / GLA_TAU)
        halo, states = _mix_rows(z, log_a, halo, states, t * rows + r * sub, wpool_ref, pscale_ref, gng_ref,
                                 o_ref, r * sub)
    for h in range(GLA_HEADS):
        state_ref[h] = states[h]
    halo_ref[...] = halo


def _even_front(u, w_main, w_lr, w_gate, b_gate, w_pool, pool_scale, gla_norm_g):
    bsz, seq, d = u.shape
    n = w_main.shape[1]
    hk = w_gate.shape[1]
    rows = min(1024, seq)
    sub = min(256, rows)

    def const(shape):
        return pl.BlockSpec(shape, lambda b, t: (0,) * len(shape))

    vmem = (_nbytes((d, n), BF16) + 2 * (_nbytes((rows, d), BF16) + _nbytes((rows, D_MODEL), BF16))
            + 2 * _nbytes((sub, n), F32) + 24 * _nbytes((sub, POOL_WIDTH), F32) + 4 * 2**20)
    return pl.pallas_call(
        functools.partial(_even_front_kernel, sub=sub),
        grid=(bsz, seq // rows),
        in_specs=[
            pl.BlockSpec((1, rows, d), lambda b, t: (b, t, 0)),
            pl.BlockSpec((d, n), lambda b, t: (0, 0), pipeline_mode=pl.Buffered(1)),
            const((d, LANES)), const((LANES, hk)), const((1, hk)),
            const((len(POOL_WINDOWS), POOL_GROUP, POOL_GROUP)), const((1, POOL_WIDTH)),
            const((1, GLA_HEADS * GLA_DV)),
        ],
        out_specs=pl.BlockSpec((1, rows, D_MODEL), lambda b, t: (b, t, 0)),
        out_shape=jax.ShapeDtypeStruct((bsz, seq, D_MODEL), BF16),
        scratch_shapes=[pltpu.VMEM((GLA_HEADS, GLA_DV, GLA_DK), F32), pltpu.VMEM((HALO, POOL_WIDTH), F32)],
        compiler_params=_cparams(("arbitrary", "arbitrary"), vmem),
        name="even_front",
    )(u, w_main, w_lr, w_gate, b_gate, w_pool, pool_scale, gla_norm_g)


def _attn_kernel(diag_ref, q_ref, k0_ref, k1_ref, k2_ref, v0_ref, v1_ref, v2_ref, o_ref, bias_ref):
    hb = diag_ref.shape[0]

    @pl.when((pl.program_id(1) == 0) & (pl.program_id(2) == 0))
    def _():
        qpos = lax.broadcasted_iota(jnp.int32, (ATT_Q, ATT_WIN), 0)
        kpos = lax.broadcasted_iota(jnp.int32, (ATT_Q, ATT_WIN), 1)
        lag = kpos // CHUNK - qpos // CHUNK
        in_band = (lag >= 0) & (lag <= ATT_PAST_CHUNKS)
        for h in range(hb):
            rows = jnp.broadcast_to(diag_ref[h], (ATT_Q, ATT_DIAG))
            table = pltpu.roll(rows, 0, 1, stride=1, stride_axis=0)[:, :ATT_WIN] * LOG2_E
            table = jnp.where(in_band, table, NEG_INF)
            for step in range(ATT_KEY_BLOCKS):
                lead = (ATT_KEY_BLOCKS - 1 - step) * ATT_Q
                bias_ref[step, h] = jnp.where(kpos >= lead, table, NEG_INF)

    variant = jnp.minimum(pl.program_id(2), ATT_KEY_BLOCKS - 1)
    k_refs = (k0_ref, k1_ref, k2_ref)
    v_refs = (v0_ref, v1_ref, v2_ref)
    nt = (((1,), (1,)), ((), ()))
    def head_cols(h):
        return slice(h * ATT_HEAD_DIM, (h + 1) * ATT_HEAD_DIM)

    def qk(h):
        q = q_ref[0, :, head_cols(h)]
        return jnp.concatenate(
            [lax.dot_general(q, kr[0, :, head_cols(h)], nt, preferred_element_type=F32) for kr in k_refs],
            axis=1)

    def softmax(h, s):
        s = s + bias_ref[variant, h]
        p = jnp.exp2(s - jnp.max(s, axis=-1, keepdims=True))
        return p.astype(BF16), jnp.sum(p, axis=-1, keepdims=True)

    def pv(h, p, denom):
        o = jnp.dot(p[:, 0:ATT_Q], v_refs[0][0, :, head_cols(h)], preferred_element_type=F32)
        for j in range(1, ATT_KEY_BLOCKS):
            o = o + jnp.dot(p[:, j * ATT_Q:(j + 1) * ATT_Q], v_refs[j][0, :, head_cols(h)],
                            preferred_element_type=F32)
        o_ref[0, :, head_cols(h)] = (o / denom).astype(BF16)

    scores, probs = {}, {}
    for t in range(hb + 2):
        if t < hb:
            scores[t] = qk(t)
        if 0 <= t - 1 < hb:
            probs[t - 1] = softmax(t - 1, scores.pop(t - 1))
        if 0 <= t - 2 < hb:
            pv(t - 2, *probs.pop(t - 2))


def _band_attention(qkv, diag):
    bsz, seq, _ = qkv.shape
    hb = ATT_HB
    width = hb * ATT_HEAD_DIM
    groups = ATT_HEADS // hb
    blk = (1, ATT_Q, width)

    def key_spec(col0, back):
        return pl.BlockSpec(blk, lambda g, b, i: (b, jnp.maximum(i - back, 0), col0 + g))

    vmem = (2 * 8 * _nbytes(blk, BF16) + _nbytes((ATT_KEY_BLOCKS, hb, ATT_Q, ATT_WIN), F32)
            + 8 * _nbytes((ATT_Q, ATT_WIN), F32) + 4 * 2**20)
    return pl.pallas_call(
        _attn_kernel,
        grid=(groups, bsz, seq // ATT_Q),
        in_specs=[
            pl.BlockSpec((hb, 1, ATT_DIAG), lambda g, b, i: (g, 0, 0)),
            pl.BlockSpec(blk, lambda g, b, i: (b, i, g)),
            key_spec(groups, 2), key_spec(groups, 1), key_spec(groups, 0),
            key_spec(2 * groups, 2), key_spec(2 * groups, 1), key_spec(2 * groups, 0),
        ],
        out_specs=pl.BlockSpec(blk, lambda g, b, i: (b, i, g)),
        out_shape=jax.ShapeDtypeStruct((bsz, seq, D_MODEL), BF16),
        scratch_shapes=[pltpu.VMEM((ATT_KEY_BLOCKS, hb, ATT_Q, ATT_WIN), F32)],
        compiler_params=_cparams(("arbitrary", "arbitrary", "arbitrary"), vmem),
        name="band_attention",
    )(diag, qkv, qkv, qkv, qkv, qkv, qkv, qkv)


def _attention_bias_diagonal(rel_bias):
    pad = ATT_PAST_CHUNKS * CHUNK
    m = jnp.arange(ATT_DIAG)
    q_minus_k = jnp.where(m < ATT_WIN, -m, ATT_DIAG - m)
    rel = jnp.clip(pad + q_minus_k, -REL_CLIP, REL_CLIP) + REL_CLIP
    return rel_bias[:, rel].astype(F32)[:, None, :]


def _mm_ln_kernel(a_ref, w_ref, x_ref, gate_ref, lng_ref, lnb_ref, *rest, alpha, emit_u, sub):
    if emit_u:
        sc_ref, sh_ref, xo_ref, uo_ref = rest
    else:
        (xo_ref,) = rest
    for r in range(a_ref.shape[1] // sub):
        rows = slice(r * sub, (r + 1) * sub)
        y = jnp.dot(a_ref[0, rows, :], w_ref[...], preferred_element_type=F32)
        t = alpha * x_ref[0, rows, :] + (1.0 + gate_ref[0]) * y
        mu = jnp.mean(t, axis=-1, keepdims=True)
        tc = t - mu
        var = jnp.mean(tc * tc, axis=-1, keepdims=True)
        xn = (tc * lax.rsqrt(var + LN_EPS)) * lng_ref[...] + lnb_ref[...]
        xo_ref[0, rows, :] = xn
        if emit_u:
            uo_ref[0, rows, :] = (xn * (1.0 + sc_ref[0]) + sh_ref[0]).astype(BF16)


def _mm_ln(a, w, x, gate, ln_g, ln_b, next_sc, next_sh, alpha, bm, name):
    bsz, seq, kdim = a.shape
    d = w.shape[1]
    bm = min(bm, seq)
    sub = min(256, bm)
    emit_u = next_sc is not None
    vec = pl.BlockSpec((1, 1, d), lambda b, i: (b, 0, 0))
    par = pl.BlockSpec((1, d), lambda b, i: (0, 0))
    row = pl.BlockSpec((1, bm, d), lambda b, i: (b, i, 0))
    in_specs = [
        pl.BlockSpec((1, bm, kdim), lambda b, i: (b, i, 0)),
        pl.BlockSpec((kdim, d), lambda b, i: (0, 0), pipeline_mode=pl.Buffered(1)),
        row, vec, par, par,
    ]
    args = [a, w, x, gate, ln_g, ln_b]
    out_specs = [row]
    out_shape = [jax.ShapeDtypeStruct((bsz, seq, d), F32)]
    if emit_u:
        in_specs += [vec, vec]
        args += [next_sc, next_sh]
        out_specs.append(row)
        out_shape.append(jax.ShapeDtypeStruct((bsz, seq, d), BF16))
    vmem = (_nbytes((kdim, d), BF16)
            + 2 * (_nbytes((bm, kdim), BF16) + 2 * _nbytes((bm, d), F32) + _nbytes((bm, d), BF16))
            + 6 * _nbytes((sub, d), F32) + 2 * 2**20)
    outs = pl.pallas_call(
        functools.partial(_mm_ln_kernel, alpha=alpha, emit_u=emit_u, sub=sub),
        grid=(bsz, seq // bm),
        in_specs=in_specs,
        out_specs=out_specs,
        out_shape=out_shape,
        compiler_params=_cparams(("arbitrary", "arbitrary"), vmem),
        name=name,
    )(*args)
    return (outs[0], outs[1]) if emit_u else (outs[0], None)


def _ffn_up_kernel(u_ref, wa_ref, wg_ref, cw_ref, cb_ref, h_ref, tail_ref, *, sub):
    @pl.when(pl.program_id(2) == 0)
    def _():
        tail_ref[...] = jnp.zeros(tail_ref.shape, F32)

    tail = tail_ref[...]
    row = lax.broadcasted_iota(jnp.int32, (sub, 1), 0)
    for r in range(u_ref.shape[1] // sub):
        rows = slice(r * sub, (r + 1) * sub)
        u = u_ref[0, rows, :]
        prev1 = tail[CONV_TAIL - 1:CONV_TAIL]
        prev2 = tail[CONV_TAIL - 2:CONV_TAIL - 1]
        g = jnp.dot(u, wg_ref[...], preferred_element_type=F32)
        g1 = jnp.where(row == 0, prev1, pltpu.roll(g, 1, 0))
        g2 = jnp.where(row == 0, prev2, jnp.where(row == 1, prev1, pltpu.roll(g, 2, 0)))
        gate = jax.nn.gelu(cb_ref[...] + g2 * cw_ref[0:1] + g1 * cw_ref[1:2] + g * cw_ref[2:3])
        a = jnp.dot(u, wa_ref[...], preferred_element_type=F32)
        h_ref[0, rows, :] = (gate * a).astype(BF16)
        tail = g[sub - CONV_TAIL:sub]
    tail_ref[...] = tail


def _ffn_up(u, w_up, conv_w, conv_b):
    bsz, seq, d = u.shape
    bm = min(2048, seq)
    sub = min(1024, bm)
    bn = FFN_TILE
    n_tiles = FFN_PAD // bn
    vmem = (2 * (_nbytes((bm, d), BF16) + 2 * _nbytes((d, bn), BF16) + _nbytes((bm, bn), BF16))
            + 8 * _nbytes((sub, bn), F32) + 2 * 2**20)
    return pl.pallas_call(
        functools.partial(_ffn_up_kernel, sub=sub),
        grid=(bsz, n_tiles, seq // bm),
        in_specs=[
            pl.BlockSpec((1, bm, d), lambda b, j, i: (b, i, 0)),
            pl.BlockSpec((d, bn), lambda b, j, i: (0, j)),
            pl.BlockSpec((d, bn), lambda b, j, i: (0, n_tiles + j)),
            pl.BlockSpec((3, bn), lambda b, j, i: (0, j)),
            pl.BlockSpec((1, bn), lambda b, j, i: (0, j)),
        ],
        out_specs=pl.BlockSpec((1, bm, bn), lambda b, j, i: (b, i, j)),
        out_shape=jax.ShapeDtypeStruct((bsz, seq, FFN_PAD), BF16),
        scratch_shapes=[pltpu.VMEM((CONV_TAIL, bn), F32)],
        compiler_params=_cparams(("arbitrary", "arbitrary", "arbitrary"), vmem),
        name="ffn_up",
    )(u, w_up, w_up, conv_w, conv_b)


def kernel(x, c, w_ada, b_ada, ln_g, ln_b, w_in_ab, w_gate_lr, b_gate, gla_norm_g, w_pool, pool_scale,
           w_out_ab, w_qkv, rel_bias, w_o, w_up, conv_w, conv_b, w_down):
    bsz, seq, d = x.shape
    depth = w_ada.shape[0]
    alpha = (2.0 * depth) ** 0.25
    fpad = FFN_PAD - FFN_DIM

    mod = _ada_modulation(c, w_ada, b_ada).reshape(depth, bsz, 6, 1, d)

    def mod_vec(layer, idx):
        return mod[layer, :, idx]

    u = _modulate(x, mod_vec(0, 1), mod_vec(0, 0))
    for layer in range(depth):
        if layer % 2 == 0:
            e = layer // 2
            w_main = w_in_ab[e, :, :OFF_LR].astype(BF16)
            w_lr = jnp.pad(w_in_ab[e, :, OFF_LR:], ((0, 0), (0, LANES - GLA_GATE_RANK))).astype(BF16)
            w_gate = jnp.pad(w_gate_lr[e], ((0, LANES - GLA_GATE_RANK), (0, 0))).astype(BF16)
            y = _even_front(u, w_main, w_lr, w_gate, b_gate[e][None], w_pool[e].astype(BF16),
                            pool_scale[e][None], gla_norm_g[e][None])
            w_proj = w_out_ab[e].astype(BF16)
        else:
            o = layer // 2
            col_scale = jnp.where(jnp.arange(3 * d) < d, ATT_HEAD_DIM ** -0.5 * LOG2_E, 1.0).astype(F32)
            w_att = (w_qkv[o] * col_scale[None, :]).astype(BF16)
            qkv = _matmul(u.reshape(bsz * seq, d), w_att, "qkv_proj").reshape(bsz, seq, 3 * d)
            y = _band_attention(qkv, _attention_bias_diagonal(rel_bias[o]))
            w_proj = w_o[o].astype(BF16)
        x, u = _mm_ln(y, w_proj, x, mod_vec(layer, 2), ln_g[layer, 0][None], ln_b[layer, 0][None],
                      mod_vec(layer, 4), mod_vec(layer, 3), alpha, 512, "mixer_out_ln")

        w_up_pad = jnp.concatenate([
            jnp.pad(w_up[layer, :, :FFN_DIM], ((0, 0), (0, fpad))),
            jnp.pad(w_up[layer, :, FFN_DIM:], ((0, 0), (0, fpad)))], axis=1).astype(BF16)
        h = _ffn_up(u, w_up_pad, jnp.pad(conv_w[layer], ((0, 0), (0, fpad))),
                    jnp.pad(conv_b[layer], (0, fpad))[None])
        w_down_pad = jnp.pad(w_down[layer], ((0, fpad), (0, 0))).astype(BF16)
        last = layer == depth - 1
        x, u = _mm_ln(h, w_down_pad, x, mod_vec(layer, 5), ln_g[layer, 1][None], ln_b[layer, 1][None],
                      None if last else mod_vec(layer + 1, 1), None if last else mod_vec(layer + 1, 0),
                      alpha, 512, "ffn_down_ln")
    return x
```

```python
import functools

import jax
import jax.numpy as jnp
from jax import lax
from jax.experimental import pallas as pl
from jax.experimental.pallas import tpu as pltpu

F32 = jnp.float32
BF16 = jnp.bfloat16

D_MODEL = 2048
CHUNK = 64
POOL_WIDTH = D_MODEL // 2
POOL_WINDOWS = (2, 4, 8, 16)
POOL_GROUP = POOL_WIDTH // len(POOL_WINDOWS)
GLA_HEADS = 4
GLA_DV = (D_MODEL // 2) // GLA_HEADS
GLA_DK = GLA_DV // 2
GLA_GATE_RANK = 16
GLA_TAU = 16.0
OFF_Q = POOL_WIDTH
OFF_K = OFF_Q + GLA_HEADS * GLA_DK
OFF_V = OFF_K + GLA_HEADS * GLA_DK
OFF_R = OFF_V + GLA_HEADS * GLA_DV
OFF_LR = OFF_R + GLA_HEADS * GLA_DV
ATT_HEADS = 16
ATT_HEAD_DIM = D_MODEL // ATT_HEADS
ATT_PAST_CHUNKS = 8
REL_CLIP = 256
FFN_DIM = 5504
LN_EPS = 1e-5
NEG_INF = -1e30

V7X_VMEM_BYTES = 64 * 2**20
LANES = 128
F32_SUBLANES = 8
BF16_SUBLANES = 16
CONV_TAIL = F32_SUBLANES

FFN_TILE = 512
FFN_PAD = -(-FFN_DIM // FFN_TILE) * FFN_TILE
HALO = BF16_SUBLANES
ATT_Q = 4 * CHUNK
ATT_WIN = ATT_Q + ATT_PAST_CHUNKS * CHUNK
ATT_HB = 8
ATT_KEY_BLOCKS = ATT_WIN // ATT_Q
ATT_DIAG = ATT_WIN + ATT_Q
LOG2_E = 1.4426950408889634


def _cparams(semantics, vmem_bytes):
    limit = min(int(vmem_bytes), V7X_VMEM_BYTES - 4 * 2**20)
    return pltpu.CompilerParams(dimension_semantics=semantics, vmem_limit_bytes=limit)


def _nbytes(shape, dtype):
    n = 1
    for s in shape:
        n *= s
    return n * jnp.dtype(dtype).itemsize


def _ada_kernel(c_ref, w_ref, b_ref, o_ref):
    c = c_ref[...]
    cond = (c * jax.nn.sigmoid(c)).astype(BF16)
    o_ref[0] = jnp.dot(cond, w_ref[0].astype(BF16), preferred_element_type=F32) + b_ref[0]


def _ada_modulation(c, w_ada, b_ada):
    depth, d, n = w_ada.shape
    bsz = c.shape[0]
    rows = -(-bsz // 8) * 8
    bn = 1024
    c_pad = jnp.pad(c, ((0, rows - bsz), (0, 0)))
    vmem = 2 * (_nbytes((d, bn), F32) + _nbytes((rows, bn), F32)) + _nbytes((d, bn), BF16) + 2**20
    out = pl.pallas_call(
        _ada_kernel,
        grid=(depth, n // bn),
        in_specs=[
            pl.BlockSpec((rows, d), lambda l, j: (0, 0)),
            pl.BlockSpec((1, d, bn), lambda l, j: (l, 0, j)),
            pl.BlockSpec((1, 1, bn), lambda l, j: (l, 0, j)),
        ],
        out_specs=pl.BlockSpec((1, rows, bn), lambda l, j: (l, 0, j)),
        out_shape=jax.ShapeDtypeStruct((depth, rows, n), F32),
        compiler_params=_cparams(("arbitrary", "arbitrary"), vmem),
        name="ada_modulation",
    )(c_pad, w_ada, b_ada.reshape(depth, 1, n))
    return out[:, :bsz]


def _modulate_kernel(x_ref, sc_ref, sh_ref, u_ref):
    u_ref[0] = (x_ref[0] * (1.0 + sc_ref[0]) + sh_ref[0]).astype(BF16)


def _modulate(x, sc, sh):
    bsz, seq, d = x.shape
    bs = min(1024, seq)
    vmem = 2 * (_nbytes((bs, d), F32) + _nbytes((bs, d), BF16)) + 2**20
    return pl.pallas_call(
        _modulate_kernel,
        grid=(bsz, seq // bs),
        in_specs=[
            pl.BlockSpec((1, bs, d), lambda b, s: (b, s, 0)),
            pl.BlockSpec((1, 1, d), lambda b, s: (b, 0, 0)),
            pl.BlockSpec((1, 1, d), lambda b, s: (b, 0, 0)),
        ],
        out_specs=pl.BlockSpec((1, bs, d), lambda b, s: (b, s, 0)),
        out_shape=jax.ShapeDtypeStruct((bsz, seq, d), BF16),
        compiler_params=_cparams(("arbitrary", "arbitrary"), vmem),
        name="modulate",
    )(x, sc, sh)


def _matmul_kernel(a_ref, w_ref, o_ref):
    o_ref[...] = jnp.dot(a_ref[...], w_ref[...], preferred_element_type=F32).astype(o_ref.dtype)


def _matmul(a, w_stack, idx, name):
    m, k = a.shape
    n = w_stack.shape[2]
    bm = min(1024, m)
    bn = min(2048, n)
    vmem = (2 * (_nbytes((bm, k), BF16) + _nbytes((k, bn), BF16) + _nbytes((bm, bn), BF16))
            + _nbytes((bm, bn), F32) + 2**20)
    return pl.pallas_call(
        _matmul_kernel,
        grid=(m // bm, n // bn),
        in_specs=[
            pl.BlockSpec((bm, k), lambda i, j: (i, 0)),
            pl.BlockSpec((None, k, bn), lambda i, j: (idx, 0, j)),
        ],
        out_specs=pl.BlockSpec((bm, bn), lambda i, j: (i, j)),
        out_shape=jax.ShapeDtypeStruct((m, n), BF16),
        compiler_params=_cparams(("arbitrary", "arbitrary"), vmem),
        name=name,
    )(a, w_stack)


def _mix_rows(z, log_a, halo, states, first_row, wpool_ref, pscale_ref, gng_ref, o_ref, out_row):
    rows = z.shape[0]
    n_chunks = rows // CHUNK

    pooled = z[:, 0:POOL_WIDTH].astype(F32)
    ext = jnp.concatenate([halo, pooled], axis=0)
    grow = lax.broadcasted_iota(jnp.int32, (rows, 1), 0) + first_row
    for gi, w in enumerate(POOL_WINDOWS):
        cols = slice(gi * POOL_GROUP, (gi + 1) * POOL_GROUP)
        e = ext[:, cols]
        s = e
        step = 1
        while step < w:
            s = s + pltpu.roll(s, step, 0)
            step *= 2
        count = jnp.minimum(grow + 1, w).astype(F32)
        dev = s[HALO:] / count - e[HALO:]
        y_a = jnp.dot(dev.astype(BF16), wpool_ref[gi], preferred_element_type=F32) * pscale_ref[:, cols]
        o_ref[0, out_row:out_row + rows, cols] = y_a.astype(BF16)

    in_chunk = lax.broadcasted_iota(jnp.int32, (rows, 1), 0) % CHUNK
    b_all = log_a
    step = 1
    while step < CHUNK:
        b_all = b_all + jnp.where(in_chunk >= step, pltpu.roll(b_all, step, 0), 0.0)
        step *= 2
    q_all = z[:, OFF_Q:OFF_K].astype(F32)
    k_all = z[:, OFF_K:OFF_V].astype(F32)
    q_dec = ((q_all * (GLA_DK ** -0.5)) * jnp.exp(b_all)).astype(BF16)
    k_inv = (k_all * jnp.exp(-b_all)).astype(BF16)
    causal = (lax.broadcasted_iota(jnp.int32, (CHUNK, CHUNK), 0)
              >= lax.broadcasted_iota(jnp.int32, (CHUNK, CHUNK), 1))
    nt = (((1,), (1,)), ((), ()))
    units = [(c, h) for c in range(n_chunks) for h in range(GLA_HEADS)]

    def cut(c, h):
        return (slice(c * CHUNK, (c + 1) * CHUNK), slice(h * GLA_DK, (h + 1) * GLA_DK),
                slice(h * GLA_DV, (h + 1) * GLA_DV))

    intra, update, decay = {}, {}, {}
    for c in range(n_chunks):
        rs = slice(c * CHUNK, (c + 1) * CHUNK)
        b_c = b_all[rs]
        b_last = b_c[CHUNK - 1:CHUNK]
        k_dec = (k_all[rs] * jnp.exp(b_last - b_c)).astype(BF16)
        decay_c = jnp.exp(b_last)
        for h in range(GLA_HEADS):
            _, ks, vs = cut(c, h)
            vh = z[rs, OFF_V + h * GLA_DV:OFF_V + (h + 1) * GLA_DV]
            scores = lax.dot_general(q_dec[rs, ks], k_inv[rs, ks], nt, preferred_element_type=F32)
            scores = jnp.where(causal, scores, 0.0).astype(BF16)
            intra[c, h] = jnp.dot(scores, vh, preferred_element_type=F32)
            update[c, h] = lax.dot_general(vh, k_dec[:, ks], (((0,), (0,)), ((), ())),
                                           preferred_element_type=F32)
            decay[c, h] = decay_c[:, ks]
    before = {}
    new_states = []
    for h in range(GLA_HEADS):
        state = states[h]
        for c in range(n_chunks):
            before[c, h] = state.astype(BF16)
            state = decay[c, h] * state + update[c, h]
        new_states.append(state)
    for c, h in units:
        rs, ks, vs = cut(c, h)
        o = intra[c, h] + lax.dot_general(q_dec[rs, ks], before[c, h], nt, preferred_element_type=F32)
        mu = jnp.mean(o, axis=-1, keepdims=True)
        oc = o - mu
        var = jnp.mean(oc * oc, axis=-1, keepdims=True)
        o_n = (oc * lax.rsqrt(var + LN_EPS)) * gng_ref[:, vs]
        r_h = z[rs, OFF_R + h * GLA_DV:OFF_R + (h + 1) * GLA_DV].astype(F32)
        y_b = (r_h * jax.nn.sigmoid(r_h)) * o_n
        o_ref[0, out_row + c * CHUNK:out_row + (c + 1) * CHUNK,
              POOL_WIDTH + h * GLA_DV:POOL_WIDTH + (h + 1) * GLA_DV] = y_b.astype(BF16)
    return pooled[rows - HALO:rows], new_states


def _even_front_kernel(u_ref, w_ref, wlr_ref, wg_ref, bg_ref, wpool_ref, pscale_ref, gng_ref, o_ref,
                       state_ref, halo_ref, *, sub):
    t = pl.program_id(1)
    rows = u_ref.shape[1]

    @pl.when(t == 0)
    def _():
        state_ref[...] = jnp.zeros(state_ref.shape, F32)
        halo_ref[...] = jnp.zeros(halo_ref.shape, F32)

    states = [state_ref[h] for h in range(GLA_HEADS)]
    halo = halo_ref[...]
    for r in range(rows // sub):
        u = u_ref[0, r * sub:(r + 1) * sub, :]
        z = jnp.dot(u, w_ref[...], preferred_element_type=F32).astype(BF16)
        z_lr = jnp.dot(u, wlr_ref[...], preferred_element_type=F32)
        lr_col = lax.broadcasted_iota(jnp.int32, z_lr.shape, 1)
        z_lr = jnp.where(lr_col < GLA_GATE_RANK, z_lr, 0.0).astype(BF16)
        pre = jnp.dot(z_lr, wg_ref[...], preferred_element_type=F32) + bg_ref[...]
        log_a = (jnp.minimum(pre, 0.0) - jnp.log1p(jnp.exp(-jnp.abs(pre)))) * (1.0 / GLA_TAU)
        halo, states = _mix_rows(z, log_a, halo, states, t * rows + r * sub, wpool_ref, pscale_ref, gng_ref,
                                 o_ref, r * sub)
    for h in range(GLA_HEADS):
        state_ref[h] = states[h]
    halo_ref[...] = halo


def _even_front(u, e, w_in, w_gate, b_gate, w_pool, pool_scale, gla_norm_g):
    bsz, seq, d = u.shape
    hk = w_gate.shape[2]
    rows = min(1024, seq)
    sub = min(256, rows)

    def layer(shape, *tail):
        return pl.BlockSpec((None,) + shape, lambda b, t: (e,) + (tail or (0,) * len(shape)))

    vmem = (_nbytes((d, OFF_LR), BF16) + 2 * (_nbytes((rows, d), BF16) + _nbytes((rows, D_MODEL), BF16))
            + 2 * _nbytes((sub, OFF_LR), F32) + 24 * _nbytes((sub, POOL_WIDTH), F32) + 4 * 2**20)
    return pl.pallas_call(
        functools.partial(_even_front_kernel, sub=sub),
        grid=(bsz, seq // rows),
        in_specs=[
            pl.BlockSpec((1, rows, d), lambda b, t: (b, t, 0)),
            pl.BlockSpec((None, d, OFF_LR), lambda b, t: (e, 0, 0), pipeline_mode=pl.Buffered(1)),
            layer((d, LANES), 0, OFF_LR // LANES),
            layer((LANES, hk)), layer((1, hk)),
            layer((len(POOL_WINDOWS), POOL_GROUP, POOL_GROUP)), layer((1, POOL_WIDTH)),
            layer((1, GLA_HEADS * GLA_DV)),
        ],
        out_specs=pl.BlockSpec((1, rows, D_MODEL), lambda b, t: (b, t, 0)),
        out_shape=jax.ShapeDtypeStruct((bsz, seq, D_MODEL), BF16),
        scratch_shapes=[pltpu.VMEM((GLA_HEADS, GLA_DV, GLA_DK), F32), pltpu.VMEM((HALO, POOL_WIDTH), F32)],
        compiler_params=_cparams(("arbitrary", "arbitrary"), vmem),
        name="even_front",
    )(u, w_in, w_in, w_gate, b_gate, w_pool, pool_scale, gla_norm_g)


def _attn_kernel(diag_ref, q_ref, k0_ref, k1_ref, k2_ref, v0_ref, v1_ref, v2_ref, o_ref, bias_ref):
    hb = diag_ref.shape[0]

    @pl.when((pl.program_id(1) == 0) & (pl.program_id(2) == 0))
    def _():
        qpos = lax.broadcasted_iota(jnp.int32, (ATT_Q, ATT_WIN), 0)
        kpos = lax.broadcasted_iota(jnp.int32, (ATT_Q, ATT_WIN), 1)
        lag = kpos // CHUNK - qpos // CHUNK
        in_band = (lag >= 0) & (lag <= ATT_PAST_CHUNKS)
        for h in range(hb):
            rows = jnp.broadcast_to(diag_ref[h], (ATT_Q, ATT_DIAG))
            table = pltpu.roll(rows, 0, 1, stride=1, stride_axis=0)[:, :ATT_WIN] * LOG2_E
            table = jnp.where(in_band, table, NEG_INF)
            for step in range(ATT_KEY_BLOCKS):
                lead = (ATT_KEY_BLOCKS - 1 - step) * ATT_Q
                bias_ref[step, h] = jnp.where(kpos >= lead, table, NEG_INF)

    variant = jnp.minimum(pl.program_id(2), ATT_KEY_BLOCKS - 1)
    k_refs = (k0_ref, k1_ref, k2_ref)
    v_refs = (v0_ref, v1_ref, v2_ref)
    nt = (((1,), (1,)), ((), ()))
    def head_cols(h):
        return slice(h * ATT_HEAD_DIM, (h + 1) * ATT_HEAD_DIM)

    def qk(h):
        q = q_ref[0, :, head_cols(h)]
        return jnp.concatenate(
            [lax.dot_general(q, kr[0, :, head_cols(h)], nt, preferred_element_type=F32) for kr in k_refs],
            axis=1)

    def softmax(h, s):
        s = s + bias_ref[variant, h]
        p = jnp.exp2(s - jnp.max(s, axis=-1, keepdims=True))
        return p.astype(BF16), jnp.sum(p, axis=-1, keepdims=True)

    def pv(h, p, denom):
        o = jnp.dot(p[:, 0:ATT_Q], v_refs[0][0, :, head_cols(h)], preferred_element_type=F32)
        for j in range(1, ATT_KEY_BLOCKS):
            o = o + jnp.dot(p[:, j * ATT_Q:(j + 1) * ATT_Q], v_refs[j][0, :, head_cols(h)],
                            preferred_element_type=F32)
        o_ref[0, :, head_cols(h)] = (o / denom).astype(BF16)

    scores, probs = {}, {}
    for t in range(hb + 2):
        if t < hb:
            scores[t] = qk(t)
        if 0 <= t - 1 < hb:
            probs[t - 1] = softmax(t - 1, scores.pop(t - 1))
        if 0 <= t - 2 < hb:
            pv(t - 2, *probs.pop(t - 2))


def _band_attention(qkv, diag):
    bsz, seq, _ = qkv.shape
    hb = ATT_HB
    width = hb * ATT_HEAD_DIM
    groups = ATT_HEADS // hb
    blk = (1, ATT_Q, width)

    def key_spec(col0, back):
        return pl.BlockSpec(blk, lambda g, b, i: (b, jnp.maximum(i - back, 0), col0 + g))

    vmem = (2 * 8 * _nbytes(blk, BF16) + _nbytes((ATT_KEY_BLOCKS, hb, ATT_Q, ATT_WIN), F32)
            + 8 * _nbytes((ATT_Q, ATT_WIN), F32) + 4 * 2**20)
    return pl.pallas_call(
        _attn_kernel,
        grid=(groups, bsz, seq // ATT_Q),
        in_specs=[
            pl.BlockSpec((hb, 1, ATT_DIAG), lambda g, b, i: (g, 0, 0)),
            pl.BlockSpec(blk, lambda g, b, i: (b, i, g)),
            key_spec(groups, 2), key_spec(groups, 1), key_spec(groups, 0),
            key_spec(2 * groups, 2), key_spec(2 * groups, 1), key_spec(2 * groups, 0),
        ],
        out_specs=pl.BlockSpec(blk, lambda g, b, i: (b, i, g)),
        out_shape=jax.ShapeDtypeStruct((bsz, seq, D_MODEL), BF16),
        scratch_shapes=[pltpu.VMEM((ATT_KEY_BLOCKS, hb, ATT_Q, ATT_WIN), F32)],
        compiler_params=_cparams(("arbitrary", "arbitrary", "arbitrary"), vmem),
        name="band_attention",
    )(diag, qkv, qkv, qkv, qkv, qkv, qkv, qkv)


def _attention_bias_diagonal(rel_bias):
    pad = ATT_PAST_CHUNKS * CHUNK
    m = jnp.arange(ATT_DIAG)
    q_minus_k = jnp.where(m < ATT_WIN, -m, ATT_DIAG - m)
    rel = jnp.clip(pad + q_minus_k, -REL_CLIP, REL_CLIP) + REL_CLIP
    return rel_bias[:, rel].astype(F32)[:, None, :]


def _mm_ln_kernel(a_ref, w_ref, x_ref, gate_ref, lng_ref, lnb_ref, *rest, alpha, emit_u, sub):
    if emit_u:
        sc_ref, sh_ref, xo_ref, uo_ref = rest
    else:
        (xo_ref,) = rest
    for r in range(a_ref.shape[1] // sub):
        rows = slice(r * sub, (r + 1) * sub)
        y = jnp.dot(a_ref[0, rows, :], w_ref[...], preferred_element_type=F32)
        t = alpha * x_ref[0, rows, :] + (1.0 + gate_ref[0]) * y
        mu = jnp.mean(t, axis=-1, keepdims=True)
        tc = t - mu
        var = jnp.mean(tc * tc, axis=-1, keepdims=True)
        xn = (tc * lax.rsqrt(var + LN_EPS)) * lng_ref[...] + lnb_ref[...]
        xo_ref[0, rows, :] = xn
        if emit_u:
            uo_ref[0, rows, :] = (xn * (1.0 + sc_ref[0]) + sh_ref[0]).astype(BF16)


def _mm_ln(a, w_stack, idx, x, gate, ln_g, ln_b, next_sc, next_sh, alpha, bm, name):
    bsz, seq, kdim = a.shape
    d = w_stack.shape[2]
    bm = min(bm, seq)
    sub = min(256, bm)
    emit_u = next_sc is not None
    vec = pl.BlockSpec((1, 1, d), lambda b, i: (b, 0, 0))
    par = pl.BlockSpec((1, d), lambda b, i: (0, 0))
    row = pl.BlockSpec((1, bm, d), lambda b, i: (b, i, 0))
    in_specs = [
        pl.BlockSpec((1, bm, kdim), lambda b, i: (b, i, 0)),
        pl.BlockSpec((None, kdim, d), lambda b, i: (idx, 0, 0), pipeline_mode=pl.Buffered(1)),
        row, vec, par, par,
    ]
    args = [a, w_stack, x, gate, ln_g, ln_b]
    out_specs = [row]
    out_shape = [jax.ShapeDtypeStruct((bsz, seq, d), F32)]
    if emit_u:
        in_specs += [vec, vec]
        args += [next_sc, next_sh]
        out_specs.append(row)
        out_shape.append(jax.ShapeDtypeStruct((bsz, seq, d), BF16))
    vmem = (_nbytes((kdim, d), BF16)
            + 2 * (_nbytes((bm, kdim), BF16) + 2 * _nbytes((bm, d), F32) + _nbytes((bm, d), BF16))
            + 6 * _nbytes((sub, d), F32) + 2 * 2**20)
    outs = pl.pallas_call(
        functools.partial(_mm_ln_kernel, alpha=alpha, emit_u=emit_u, sub=sub),
        grid=(bsz, seq // bm),
        in_specs=in_specs,
        out_specs=out_specs,
        out_shape=out_shape,
        compiler_params=_cparams(("arbitrary", "arbitrary"), vmem),
        name=name,
    )(*args)
    return (outs[0], outs[1]) if emit_u else (outs[0], None)


def _ffn_up_kernel(u_ref, wa_ref, wg_ref, cw_ref, cb_ref, h_ref, tail_ref, *, sub):
    @pl.when(pl.program_id(2) == 0)
    def _():
        tail_ref[...] = jnp.zeros(tail_ref.shape, F32)

    tail = tail_ref[...]
    row = lax.broadcasted_iota(jnp.int32, (sub, 1), 0)
    for r in range(u_ref.shape[1] // sub):
        rows = slice(r * sub, (r + 1) * sub)
        u = u_ref[0, rows, :]
        prev1 = tail[CONV_TAIL - 1:CONV_TAIL]
        prev2 = tail[CONV_TAIL - 2:CONV_TAIL - 1]
        g = jnp.dot(u, wg_ref[...], preferred_element_type=F32)
        g1 = jnp.where(row == 0, prev1, pltpu.roll(g, 1, 0))
        g2 = jnp.where(row == 0, prev2, jnp.where(row == 1, prev1, pltpu.roll(g, 2, 0)))
        gate = jax.nn.gelu(cb_ref[...] + g2 * cw_ref[0:1] + g1 * cw_ref[1:2] + g * cw_ref[2:3])
        a = jnp.dot(u, wa_ref[...], preferred_element_type=F32)
        h_ref[0, rows, :] = (gate * a).astype(BF16)
        tail = g[sub - CONV_TAIL:sub]
    tail_ref[...] = tail


def _ffn_up(u, layer, w_a, w_g, conv_w, conv_b):
    bsz, seq, d = u.shape
    bm = min(2048, seq)
    sub = min(1024, bm)
    bn = FFN_TILE
    n_tiles = FFN_PAD // bn
    vmem = (2 * (_nbytes((bm, d), BF16) + 2 * _nbytes((d, bn), BF16) + _nbytes((bm, bn), BF16))
            + 8 * _nbytes((sub, bn), F32) + 2 * 2**20)
    return pl.pallas_call(
        functools.partial(_ffn_up_kernel, sub=sub),
        grid=(bsz, n_tiles, seq // bm),
        in_specs=[
            pl.BlockSpec((1, bm, d), lambda b, j, i: (b, i, 0)),
            pl.BlockSpec((None, d, bn), lambda b, j, i: (layer, 0, j)),
            pl.BlockSpec((None, d, bn), lambda b, j, i: (layer, 0, j)),
            pl.BlockSpec((None, 3, bn), lambda b, j, i: (layer, 0, j)),
            pl.BlockSpec((None, 1, bn), lambda b, j, i: (layer, 0, j)),
        ],
        out_specs=pl.BlockSpec((1, bm, bn), lambda b, j, i: (b, i, j)),
        out_shape=jax.ShapeDtypeStruct((bsz, seq, FFN_PAD), BF16),
        scratch_shapes=[pltpu.VMEM((CONV_TAIL, bn), F32)],
        compiler_params=_cparams(("arbitrary", "arbitrary", "arbitrary"), vmem),
        name="ffn_up",
    )(u, w_a, w_g, conv_w, conv_b)


def kernel(x, c, w_ada, b_ada, ln_g, ln_b, w_in_ab, w_gate_lr, b_gate, gla_norm_g, w_pool, pool_scale,
           w_out_ab, w_qkv, rel_bias, w_o, w_up, conv_w, conv_b, w_down):
    bsz, seq, d = x.shape
    depth = w_ada.shape[0]
    alpha = (2.0 * depth) ** 0.25
    fpad = FFN_PAD - FFN_DIM

    mod = _ada_modulation(c, w_ada, b_ada).reshape(depth, bsz, 6, 1, d)

    def mod_vec(layer, idx):
        return mod[layer, :, idx]

    w_in = w_in_ab.astype(BF16)
    w_gate = jnp.pad(w_gate_lr, ((0, 0), (0, LANES - GLA_GATE_RANK), (0, 0))).astype(BF16)
    w_pool_b = w_pool.astype(BF16)
    w_out = w_out_ab.astype(BF16)
    col_scale = jnp.where(jnp.arange(3 * d) < d, ATT_HEAD_DIM ** -0.5 * LOG2_E, 1.0).astype(F32)
    w_att = (w_qkv * col_scale[None, None, :]).astype(BF16)
    w_o_b = w_o.astype(BF16)
    w_a = jnp.pad(w_up[:, :, :FFN_DIM], ((0, 0), (0, 0), (0, fpad))).astype(BF16)
    w_g = jnp.pad(w_up[:, :, FFN_DIM:], ((0, 0), (0, 0), (0, fpad))).astype(BF16)
    conv_w_pad = jnp.pad(conv_w, ((0, 0), (0, 0), (0, fpad)))
    conv_b_pad = jnp.pad(conv_b, ((0, 0), (0, fpad)))[:, None, :]
    w_down_pad = jnp.pad(w_down, ((0, 0), (0, fpad), (0, 0))).astype(BF16)

    u = _modulate(x, mod_vec(0, 1), mod_vec(0, 0))
    for layer in range(depth):
        half = layer // 2
        if layer % 2 == 0:
            y = _even_front(u, half, w_in, w_gate, b_gate[:, None, :], w_pool_b, pool_scale[:, None, :],
                            gla_norm_g[:, None, :])
            w_proj = w_out
        else:
            qkv = _matmul(u.reshape(bsz * seq, d), w_att, half, "qkv_proj").reshape(bsz, seq, 3 * d)
            y = _band_attention(qkv, _attention_bias_diagonal(rel_bias[half]))
            w_proj = w_o_b
        x, u = _mm_ln(y, w_proj, half, x, mod_vec(layer, 2), ln_g[layer, 0][None], ln_b[layer, 0][None],
                      mod_vec(layer, 4), mod_vec(layer, 3), alpha, 512, "mixer_out_ln")
        h = _ffn_up(u, layer, w_a, w_g, conv_w_pad, conv_b_pad)
        last = layer == depth - 1
        x, u = _mm_ln(h, w_down_pad, layer, x, mod_vec(layer, 5), ln_g[layer, 1][None], ln_b[layer, 1][None],
                      None if last else mod_vec(layer + 1, 1), None if last else mod_vec(layer + 1, 0),
                      alpha, 512, "ffn_down_ln")
    return x
```

```python
import functools

import jax
import jax.numpy as jnp
from jax import lax
from jax.experimental import pallas as pl
from jax.experimental.pallas import tpu as pltpu

F32 = jnp.float32
BF16 = jnp.bfloat16

D_MODEL = 2048
CHUNK = 64
POOL_WIDTH = D_MODEL // 2
POOL_WINDOWS = (2, 4, 8, 16)
POOL_GROUP = POOL_WIDTH // len(POOL_WINDOWS)
GLA_HEADS = 4
GLA_DV = (D_MODEL // 2) // GLA_HEADS
GLA_DK = GLA_DV // 2
GLA_GATE_RANK = 16
GLA_TAU = 16.0
OFF_Q = POOL_WIDTH
OFF_K = OFF_Q + GLA_HEADS * GLA_DK
OFF_V = OFF_K + GLA_HEADS * GLA_DK
OFF_R = OFF_V + GLA_HEADS * GLA_DV
OFF_LR = OFF_R + GLA_HEADS * GLA_DV
ATT_HEADS = 16
ATT_HEAD_DIM = D_MODEL // ATT_HEADS
ATT_PAST_CHUNKS = 8
REL_CLIP = 256
FFN_DIM = 5504
LN_EPS = 1e-5
NEG_INF = -1e30

V7X_VMEM_BYTES = 64 * 2**20
LANES = 128
F32_SUBLANES = 8
BF16_SUBLANES = 16
CONV_TAIL = F32_SUBLANES

FFN_TILE = 512
HALO = BF16_SUBLANES
ATT_Q = 4 * CHUNK
ATT_WIN = ATT_Q + ATT_PAST_CHUNKS * CHUNK
ATT_HB = 8
ATT_KEY_BLOCKS = ATT_WIN // ATT_Q
ATT_DIAG = ATT_WIN + ATT_Q
LOG2_E = 1.4426950408889634


def _cparams(semantics, vmem_bytes):
    limit = min(int(vmem_bytes), V7X_VMEM_BYTES - 4 * 2**20)
    return pltpu.CompilerParams(dimension_semantics=semantics, vmem_limit_bytes=limit)


def _nbytes(shape, dtype):
    n = 1
    for s in shape:
        n *= s
    return n * jnp.dtype(dtype).itemsize


def _ada_kernel(c_ref, w_ref, b_ref, o_ref):
    c = c_ref[...]
    cond = (c * jax.nn.sigmoid(c)).astype(BF16)
    o_ref[0] = jnp.dot(cond, w_ref[0].astype(BF16), preferred_element_type=F32) + b_ref[0]


def _ada_modulation(c, w_ada, b_ada):
    depth, d, n = w_ada.shape
    bsz = c.shape[0]
    rows = -(-bsz // 8) * 8
    bn = 1024
    c_pad = jnp.pad(c, ((0, rows - bsz), (0, 0)))
    vmem = 2 * (_nbytes((d, bn), F32) + _nbytes((rows, bn), F32)) + _nbytes((d, bn), BF16) + 2**20
    out = pl.pallas_call(
        _ada_kernel,
        grid=(depth, n // bn),
        in_specs=[
            pl.BlockSpec((rows, d), lambda l, j: (0, 0)),
            pl.BlockSpec((1, d, bn), lambda l, j: (l, 0, j)),
            pl.BlockSpec((1, 1, bn), lambda l, j: (l, 0, j)),
        ],
        out_specs=pl.BlockSpec((1, rows, bn), lambda l, j: (l, 0, j)),
        out_shape=jax.ShapeDtypeStruct((depth, rows, n), F32),
        compiler_params=_cparams(("arbitrary", "arbitrary"), vmem),
        name="ada_modulation",
    )(c_pad, w_ada, b_ada.reshape(depth, 1, n))
    return out[:, :bsz]


def _matmul_kernel(a_ref, w_ref, o_ref):
    o_ref[...] = jnp.dot(a_ref[...], w_ref[...], preferred_element_type=F32).astype(o_ref.dtype)


def _matmul(a, w_stack, idx, name):
    m, k = a.shape
    n = w_stack.shape[2]
    bm = min(1024, m)
    bn = min(2048, n)
    vmem = (2 * (_nbytes((bm, k), BF16) + _nbytes((k, bn), BF16) + _nbytes((bm, bn), BF16))
            + _nbytes((bm, bn), F32) + 2**20)
    return pl.pallas_call(
        _matmul_kernel,
        grid=(m // bm, n // bn),
        in_specs=[
            pl.BlockSpec((bm, k), lambda i, j: (i, 0)),
            pl.BlockSpec((None, k, bn), lambda i, j: (idx, 0, j)),
        ],
        out_specs=pl.BlockSpec((bm, bn), lambda i, j: (i, j)),
        out_shape=jax.ShapeDtypeStruct((m, n), BF16),
        compiler_params=_cparams(("arbitrary", "arbitrary"), vmem),
        name=name,
    )(a, w_stack)


def _mix_rows(z, log_a, halo, states, first_row, wpool_ref, pscale_ref, gng_ref, o_ref, out_row):
    rows = z.shape[0]
    n_chunks = rows // CHUNK

    pooled = z[:, 0:POOL_WIDTH].astype(F32)
    ext = jnp.concatenate([halo, pooled], axis=0)
    grow = lax.broadcasted_iota(jnp.int32, (rows, 1), 0) + first_row
    for gi, w in enumerate(POOL_WINDOWS):
        cols = slice(gi * POOL_GROUP, (gi + 1) * POOL_GROUP)
        e = ext[:, cols]
        s = e
        step = 1
        while step < w:
            s = s + pltpu.roll(s, step, 0)
            step *= 2
        count = jnp.minimum(grow + 1, w).astype(F32)
        dev = s[HALO:] / count - e[HALO:]
        y_a = jnp.dot(dev.astype(BF16), wpool_ref[gi], preferred_element_type=F32) * pscale_ref[:, cols]
        o_ref[0, out_row:out_row + rows, cols] = y_a.astype(BF16)

    in_chunk = lax.broadcasted_iota(jnp.int32, (rows, 1), 0) % CHUNK
    b_all = log_a
    step = 1
    while step < CHUNK:
        b_all = b_all + jnp.where(in_chunk >= step, pltpu.roll(b_all, step, 0), 0.0)
        step *= 2
    q_all = z[:, OFF_Q:OFF_K].astype(F32)
    k_all = z[:, OFF_K:OFF_V].astype(F32)
    q_dec = ((q_all * (GLA_DK ** -0.5)) * jnp.exp(b_all)).astype(BF16)
    k_inv = (k_all * jnp.exp(-b_all)).astype(BF16)
    causal = (lax.broadcasted_iota(jnp.int32, (CHUNK, CHUNK), 0)
              >= lax.broadcasted_iota(jnp.int32, (CHUNK, CHUNK), 1))
    nt = (((1,), (1,)), ((), ()))
    units = [(c, h) for c in range(n_chunks) for h in range(GLA_HEADS)]

    def cut(c, h):
        return (slice(c * CHUNK, (c + 1) * CHUNK), slice(h * GLA_DK, (h + 1) * GLA_DK),
                slice(h * GLA_DV, (h + 1) * GLA_DV))

    intra, update, decay = {}, {}, {}
    for c in range(n_chunks):
        rs = slice(c * CHUNK, (c + 1) * CHUNK)
        b_c = b_all[rs]
        b_last = b_c[CHUNK - 1:CHUNK]
        k_dec = (k_all[rs] * jnp.exp(b_last - b_c)).astype(BF16)
        decay_c = jnp.exp(b_last)
        for h in range(GLA_HEADS):
            _, ks, vs = cut(c, h)
            vh = z[rs, OFF_V + h * GLA_DV:OFF_V + (h + 1) * GLA_DV]
            scores = lax.dot_general(q_dec[rs, ks], k_inv[rs, ks], nt, preferred_element_type=F32)
            scores = jnp.where(causal, scores, 0.0).astype(BF16)
            intra[c, h] = jnp.dot(scores, vh, preferred_element_type=F32)
            update[c, h] = lax.dot_general(vh, k_dec[:, ks], (((0,), (0,)), ((), ())),
                                           preferred_element_type=F32)
            decay[c, h] = decay_c[:, ks]
    before = {}
    new_states = []
    for h in range(GLA_HEADS):
        state = states[h]
        for c in range(n_chunks):
            before[c, h] = state.astype(BF16)
            state = decay[c, h] * state + update[c, h]
        new_states.append(state)
    for c, h in units:
        rs, ks, vs = cut(c, h)
        o = intra[c, h] + lax.dot_general(q_dec[rs, ks], before[c, h], nt, preferred_element_type=F32)
        mu = jnp.mean(o, axis=-1, keepdims=True)
        oc = o - mu
        var = jnp.mean(oc * oc, axis=-1, keepdims=True)
        o_n = (oc * lax.rsqrt(var + LN_EPS)) * gng_ref[:, vs]
        r_h = z[rs, OFF_R + h * GLA_DV:OFF_R + (h + 1) * GLA_DV].astype(F32)
        y_b = (r_h * jax.nn.sigmoid(r_h)) * o_n
        o_ref[0, out_row + c * CHUNK:out_row + (c + 1) * CHUNK,
              POOL_WIDTH + h * GLA_DV:POOL_WIDTH + (h + 1) * GLA_DV] = y_b.astype(BF16)
    return pooled[rows - HALO:rows], new_states


def _even_front_kernel(*refs, sub, modulate):
    if modulate:
        sc_ref, sh_ref = refs[1:3]
        refs = refs[:1] + refs[3:]
    u_ref, w_ref, wlr_ref, wg_ref, bg_ref, wpool_ref, pscale_ref, gng_ref, o_ref, state_ref, halo_ref = refs
    t = pl.program_id(1)
    rows = u_ref.shape[1]

    @pl.when(t == 0)
    def _():
        state_ref[...] = jnp.zeros(state_ref.shape, F32)
        halo_ref[...] = jnp.zeros(halo_ref.shape, F32)

    states = [state_ref[h] for h in range(GLA_HEADS)]
    halo = halo_ref[...]
    for r in range(rows // sub):
        u = u_ref[0, r * sub:(r + 1) * sub, :]
        if modulate:
            u = (u * (1.0 + sc_ref[0]) + sh_ref[0]).astype(BF16)
        z = jnp.dot(u, w_ref[...], preferred_element_type=F32).astype(BF16)
        z_lr = jnp.dot(u, wlr_ref[...], preferred_element_type=F32)
        lr_col = lax.broadcasted_iota(jnp.int32, z_lr.shape, 1)
        z_lr = jnp.where(lr_col < GLA_GATE_RANK, z_lr, 0.0).astype(BF16)
        pre = jnp.dot(z_lr, wg_ref[...], preferred_element_type=F32) + bg_ref[...]
        log_a = (jnp.minimum(pre, 0.0) - jnp.log1p(jnp.exp(-jnp.abs(pre)))) * (1.0 / GLA_TAU)
        halo, states = _mix_rows(z, log_a, halo, states, t * rows + r * sub, wpool_ref, pscale_ref, gng_ref,
                                 o_ref, r * sub)
    for h in range(GLA_HEADS):
        state_ref[h] = states[h]
    halo_ref[...] = halo


def _even_front(u, e, w_in, w_gate, b_gate, w_pool, pool_scale, gla_norm_g, modulation=None):
    bsz, seq, d = u.shape
    hk = w_gate.shape[2]
    rows = min(1024, seq)
    sub = min(256, rows)

    def layer(shape, *tail):
        return pl.BlockSpec((None,) + shape, lambda b, t: (e,) + (tail or (0,) * len(shape)))

    vec = pl.BlockSpec((1, 1, d), lambda b, t: (b, 0, 0))
    lead_specs = [vec, vec] if modulation else []
    vmem = (_nbytes((d, OFF_LR), BF16) + 2 * (_nbytes((rows, d), u.dtype) + _nbytes((rows, D_MODEL), BF16))
            + 2 * _nbytes((sub, OFF_LR), F32) + 24 * _nbytes((sub, POOL_WIDTH), F32) + 4 * 2**20)
    return pl.pallas_call(
        functools.partial(_even_front_kernel, sub=sub, modulate=bool(modulation)),
        grid=(bsz, seq // rows),
        in_specs=[pl.BlockSpec((1, rows, d), lambda b, t: (b, t, 0))] + lead_specs + [
            pl.BlockSpec((None, d, OFF_LR), lambda b, t: (e, 0, 0), pipeline_mode=pl.Buffered(1)),
            layer((d, LANES), 0, OFF_LR // LANES),
            layer((LANES, hk)), layer((1, hk)),
            layer((len(POOL_WINDOWS), POOL_GROUP, POOL_GROUP)), layer((1, POOL_WIDTH)),
            layer((1, GLA_HEADS * GLA_DV)),
        ],
        out_specs=pl.BlockSpec((1, rows, D_MODEL), lambda b, t: (b, t, 0)),
        out_shape=jax.ShapeDtypeStruct((bsz, seq, D_MODEL), BF16),
        scratch_shapes=[pltpu.VMEM((GLA_HEADS, GLA_DV, GLA_DK), F32), pltpu.VMEM((HALO, POOL_WIDTH), F32)],
        compiler_params=_cparams(("arbitrary", "arbitrary"), vmem),
        name="even_front",
    )(u, *(modulation or ()), w_in, w_in, w_gate, b_gate, w_pool, pool_scale, gla_norm_g)


def _attn_kernel(diag_ref, q_ref, k0_ref, k1_ref, k2_ref, v0_ref, v1_ref, v2_ref, o_ref, bias_ref):
    hb = diag_ref.shape[0]

    @pl.when((pl.program_id(1) == 0) & (pl.program_id(2) == 0))
    def _():
        qpos = lax.broadcasted_iota(jnp.int32, (ATT_Q, ATT_WIN), 0)
        kpos = lax.broadcasted_iota(jnp.int32, (ATT_Q, ATT_WIN), 1)
        lag = kpos // CHUNK - qpos // CHUNK
        in_band = (lag >= 0) & (lag <= ATT_PAST_CHUNKS)
        for h in range(hb):
            rows = jnp.broadcast_to(diag_ref[h], (ATT_Q, ATT_DIAG))
            table = pltpu.roll(rows, 0, 1, stride=1, stride_axis=0)[:, :ATT_WIN] * LOG2_E
            table = jnp.where(in_band, table, NEG_INF)
            for step in range(ATT_KEY_BLOCKS):
                lead = (ATT_KEY_BLOCKS - 1 - step) * ATT_Q
                bias_ref[step, h] = jnp.where(kpos >= lead, table, NEG_INF)

    variant = jnp.minimum(pl.program_id(2), ATT_KEY_BLOCKS - 1)
    k_refs = (k0_ref, k1_ref, k2_ref)
    v_refs = (v0_ref, v1_ref, v2_ref)
    nt = (((1,), (1,)), ((), ()))
    def head_cols(h):
        return slice(h * ATT_HEAD_DIM, (h + 1) * ATT_HEAD_DIM)

    def qk(h):
        q = q_ref[0, :, head_cols(h)]
        return jnp.concatenate(
            [lax.dot_general(q, kr[0, :, head_cols(h)], nt, preferred_element_type=F32) for kr in k_refs],
            axis=1)

    def softmax(h, s):
        s = s + bias_ref[variant, h]
        p = jnp.exp2(s - jnp.max(s, axis=-1, keepdims=True))
        return p.astype(BF16), jnp.sum(p, axis=-1, keepdims=True)

    def pv(h, p, denom):
        o = jnp.dot(p[:, 0:ATT_Q], v_refs[0][0, :, head_cols(h)], preferred_element_type=F32)
        for j in range(1, ATT_KEY_BLOCKS):
            o = o + jnp.dot(p[:, j * ATT_Q:(j + 1) * ATT_Q], v_refs[j][0, :, head_cols(h)],
                            preferred_element_type=F32)
        o_ref[0, :, head_cols(h)] = (o / denom).astype(BF16)

    scores, probs = {}, {}
    for t in range(hb + 2):
        if t < hb:
            scores[t] = qk(t)
        if 0 <= t - 1 < hb:
            probs[t - 1] = softmax(t - 1, scores.pop(t - 1))
        if 0 <= t - 2 < hb:
            pv(t - 2, *probs.pop(t - 2))


def _band_attention(qkv, diag):
    bsz, seq, _ = qkv.shape
    hb = ATT_HB
    width = hb * ATT_HEAD_DIM
    groups = ATT_HEADS // hb
    blk = (1, ATT_Q, width)

    def key_spec(col0, back):
        return pl.BlockSpec(blk, lambda g, b, i: (b, jnp.maximum(i - back, 0), col0 + g))

    vmem = (2 * 8 * _nbytes(blk, BF16) + _nbytes((ATT_KEY_BLOCKS, hb, ATT_Q, ATT_WIN), F32)
            + 8 * _nbytes((ATT_Q, ATT_WIN), F32) + 4 * 2**20)
    return pl.pallas_call(
        _attn_kernel,
        grid=(groups, bsz, seq // ATT_Q),
        in_specs=[
            pl.BlockSpec((hb, 1, ATT_DIAG), lambda g, b, i: (g, 0, 0)),
            pl.BlockSpec(blk, lambda g, b, i: (b, i, g)),
            key_spec(groups, 2), key_spec(groups, 1), key_spec(groups, 0),
            key_spec(2 * groups, 2), key_spec(2 * groups, 1), key_spec(2 * groups, 0),
        ],
        out_specs=pl.BlockSpec(blk, lambda g, b, i: (b, i, g)),
        out_shape=jax.ShapeDtypeStruct((bsz, seq, D_MODEL), BF16),
        scratch_shapes=[pltpu.VMEM((ATT_KEY_BLOCKS, hb, ATT_Q, ATT_WIN), F32)],
        compiler_params=_cparams(("arbitrary", "arbitrary", "arbitrary"), vmem),
        name="band_attention",
    )(diag, qkv, qkv, qkv, qkv, qkv, qkv, qkv)


def _attention_bias_diagonal(rel_bias):
    pad = ATT_PAST_CHUNKS * CHUNK
    m = jnp.arange(ATT_DIAG)
    q_minus_k = jnp.where(m < ATT_WIN, -m, ATT_DIAG - m)
    rel = jnp.clip(pad + q_minus_k, -REL_CLIP, REL_CLIP) + REL_CLIP
    return rel_bias[:, rel].astype(F32)[:, None, :]


def _mm_ln_kernel(a_ref, w_ref, x_ref, gate_ref, lng_ref, lnb_ref, *rest, alpha, emit_u, sub):
    if emit_u:
        sc_ref, sh_ref, xo_ref, uo_ref = rest
    else:
        (xo_ref,) = rest
    for r in range(a_ref.shape[1] // sub):
        rows = slice(r * sub, (r + 1) * sub)
        y = jnp.dot(a_ref[0, rows, :], w_ref[...], preferred_element_type=F32)
        t = alpha * x_ref[0, rows, :] + (1.0 + gate_ref[0]) * y
        mu = jnp.mean(t, axis=-1, keepdims=True)
        tc = t - mu
        var = jnp.mean(tc * tc, axis=-1, keepdims=True)
        xn = (tc * lax.rsqrt(var + LN_EPS)) * lng_ref[...] + lnb_ref[...]
        xo_ref[0, rows, :] = xn
        if emit_u:
            uo_ref[0, rows, :] = (xn * (1.0 + sc_ref[0]) + sh_ref[0]).astype(BF16)


def _mm_ln(a, w_stack, idx, x, gate, ln_g, ln_b, next_sc, next_sh, alpha, bm, name):
    bsz, seq, kdim = a.shape
    d = w_stack.shape[2]
    bm = min(bm, seq)
    sub = min(256, bm)
    emit_u = next_sc is not None
    vec = pl.BlockSpec((1, 1, d), lambda b, i: (b, 0, 0))
    par = pl.BlockSpec((1, d), lambda b, i: (0, 0))
    row = pl.BlockSpec((1, bm, d), lambda b, i: (b, i, 0))
    in_specs = [
        pl.BlockSpec((1, bm, kdim), lambda b, i: (b, i, 0)),
        pl.BlockSpec((None, kdim, d), lambda b, i: (idx, 0, 0), pipeline_mode=pl.Buffered(1)),
        row, vec, par, par,
    ]
    args = [a, w_stack, x, gate, ln_g, ln_b]
    out_specs = [row]
    out_shape = [jax.ShapeDtypeStruct((bsz, seq, d), F32)]
    if emit_u:
        in_specs += [vec, vec]
        args += [next_sc, next_sh]
        out_specs.append(row)
        out_shape.append(jax.ShapeDtypeStruct((bsz, seq, d), BF16))
    vmem = (_nbytes((kdim, d), BF16)
            + 2 * (_nbytes((bm, kdim), BF16) + 2 * _nbytes((bm, d), F32) + _nbytes((bm, d), BF16))
            + 6 * _nbytes((sub, d), F32) + 2 * 2**20)
    outs = pl.pallas_call(
        functools.partial(_mm_ln_kernel, alpha=alpha, emit_u=emit_u, sub=sub),
        grid=(bsz, seq // bm),
        in_specs=in_specs,
        out_specs=out_specs,
        out_shape=out_shape,
        compiler_params=_cparams(("arbitrary", "arbitrary"), vmem),
        name=name,
    )(*args)
    return (outs[0], outs[1]) if emit_u else (outs[0], None)


def _ffn_up_kernel(u_ref, wa_ref, wg_ref, cw_ref, cb_ref, h_ref, tail_ref, *, sub):
    @pl.when(pl.program_id(2) == 0)
    def _():
        tail_ref[...] = jnp.zeros(tail_ref.shape, F32)

    tail = tail_ref[...]
    row = lax.broadcasted_iota(jnp.int32, (sub, 1), 0)
    for r in range(u_ref.shape[1] // sub):
        rows = slice(r * sub, (r + 1) * sub)
        u = u_ref[0, rows, :]
        prev1 = tail[CONV_TAIL - 1:CONV_TAIL]
        prev2 = tail[CONV_TAIL - 2:CONV_TAIL - 1]
        g = jnp.dot(u, wg_ref[...], preferred_element_type=F32)
        g1 = jnp.where(row == 0, prev1, pltpu.roll(g, 1, 0))
        g2 = jnp.where(row == 0, prev2, jnp.where(row == 1, prev1, pltpu.roll(g, 2, 0)))
        gate = jax.nn.gelu(cb_ref[...] + g2 * cw_ref[0:1] + g1 * cw_ref[1:2] + g * cw_ref[2:3])
        a = jnp.dot(u, wa_ref[...], preferred_element_type=F32)
        h_ref[0, rows, :] = (gate * a).astype(BF16)
        tail = g[sub - CONV_TAIL:sub]
    tail_ref[...] = tail


def _ffn_up(u, layer, w_a, w_g, conv_w, conv_b):
    bsz, seq, d = u.shape
    ffn = w_a.shape[2]
    bm = min(2048, seq)
    sub = min(1024, bm)
    bn = FFN_TILE
    n_tiles = -(-ffn // bn)
    vmem = (2 * (_nbytes((bm, d), BF16) + 2 * _nbytes((d, bn), BF16) + _nbytes((bm, bn), BF16))
            + 8 * _nbytes((sub, bn), F32) + 2 * 2**20)
    return pl.pallas_call(
        functools.partial(_ffn_up_kernel, sub=sub),
        grid=(bsz, n_tiles, seq // bm),
        in_specs=[
            pl.BlockSpec((1, bm, d), lambda b, j, i: (b, i, 0)),
            pl.BlockSpec((None, d, bn), lambda b, j, i: (layer, 0, j)),
            pl.BlockSpec((None, d, bn), lambda b, j, i: (layer, 0, j)),
            pl.BlockSpec((None, 3, bn), lambda b, j, i: (layer, 0, j)),
            pl.BlockSpec((None, 1, bn), lambda b, j, i: (layer, 0, j)),
        ],
        out_specs=pl.BlockSpec((1, bm, bn), lambda b, j, i: (b, i, j)),
        out_shape=jax.ShapeDtypeStruct((bsz, seq, ffn), BF16),
        scratch_shapes=[pltpu.VMEM((CONV_TAIL, bn), F32)],
        compiler_params=_cparams(("arbitrary", "arbitrary", "arbitrary"), vmem),
        name="ffn_up",
    )(u, w_a, w_g, conv_w, conv_b)


def kernel(x, c, w_ada, b_ada, ln_g, ln_b, w_in_ab, w_gate_lr, b_gate, gla_norm_g, w_pool, pool_scale,
           w_out_ab, w_qkv, rel_bias, w_o, w_up, conv_w, conv_b, w_down):
    bsz, seq, d = x.shape
    depth = w_ada.shape[0]
    alpha = (2.0 * depth) ** 0.25

    mod = _ada_modulation(c, w_ada, b_ada).reshape(depth, bsz, 6, 1, d)

    def mod_vec(layer, idx):
        return mod[layer, :, idx]

    w_in = w_in_ab.astype(BF16)
    w_gate = jnp.pad(w_gate_lr, ((0, 0), (0, LANES - GLA_GATE_RANK), (0, 0))).astype(BF16)
    w_pool_b = w_pool.astype(BF16)
    w_out = w_out_ab.astype(BF16)
    col_scale = jnp.where(jnp.arange(3 * d) < d, ATT_HEAD_DIM ** -0.5 * LOG2_E, 1.0).astype(F32)
    w_att = (w_qkv * col_scale[None, None, :]).astype(BF16)
    w_o_b = w_o.astype(BF16)
    w_a = w_up[:, :, :FFN_DIM].astype(BF16)
    w_g = w_up[:, :, FFN_DIM:].astype(BF16)
    w_down_b = w_down.astype(BF16)

    u = None
    for layer in range(depth):
        half = layer // 2
        if layer % 2 == 0:
            first = (x, (mod_vec(0, 1), mod_vec(0, 0))) if layer == 0 else (u, None)
            y = _even_front(first[0], half, w_in, w_gate, b_gate[:, None, :], w_pool_b,
                            pool_scale[:, None, :], gla_norm_g[:, None, :], modulation=first[1])
            w_proj = w_out
        else:
            qkv = _matmul(u.reshape(bsz * seq, d), w_att, half, "qkv_proj").reshape(bsz, seq, 3 * d)
            y = _band_attention(qkv, _attention_bias_diagonal(rel_bias[half]))
            w_proj = w_o_b
        x, u = _mm_ln(y, w_proj, half, x, mod_vec(layer, 2), ln_g[layer, 0][None], ln_b[layer, 0][None],
                      mod_vec(layer, 4), mod_vec(layer, 3), alpha, 512, "mixer_out_ln")
        h = _ffn_up(u, layer, w_a, w_g, conv_w, conv_b[:, None, :])
        last = layer == depth - 1
        x, u = _mm_ln(h, w_down_b, layer, x, mod_vec(layer, 5), ln_g[layer, 1][None], ln_b[layer, 1][None],
                      None if last else mod_vec(layer + 1, 1), None if last else mod_vec(layer + 1, 0),
                      alpha, 512, "ffn_down_ln")
    return x
```

```python
import functools

import jax
import jax.numpy as jnp
from jax import lax
from jax.experimental import pallas as pl
from jax.experimental.pallas import tpu as pltpu

F32 = jnp.float32
BF16 = jnp.bfloat16

D_MODEL = 2048
CHUNK = 64
POOL_WIDTH = D_MODEL // 2
POOL_WINDOWS = (2, 4, 8, 16)
POOL_GROUP = POOL_WIDTH // len(POOL_WINDOWS)
GLA_HEADS = 4
GLA_DV = (D_MODEL // 2) // GLA_HEADS
GLA_DK = GLA_DV // 2
GLA_GATE_RANK = 16
GLA_TAU = 16.0
OFF_Q = POOL_WIDTH
OFF_K = OFF_Q + GLA_HEADS * GLA_DK
OFF_V = OFF_K + GLA_HEADS * GLA_DK
OFF_R = OFF_V + GLA_HEADS * GLA_DV
OFF_LR = OFF_R + GLA_HEADS * GLA_DV
ATT_HEADS = 16
ATT_HEAD_DIM = D_MODEL // ATT_HEADS
ATT_PAST_CHUNKS = 8
REL_CLIP = 256
FFN_DIM = 5504
LN_EPS = 1e-5
NEG_INF = -1e30

V7X_VMEM_BYTES = 64 * 2**20
LANES = 128
F32_SUBLANES = 8
BF16_SUBLANES = 16
CONV_TAIL = F32_SUBLANES

FFN_TILE = 512
HALO = BF16_SUBLANES
ATT_Q = 4 * CHUNK
ATT_WIN = ATT_Q + ATT_PAST_CHUNKS * CHUNK
ATT_HB = 8
ATT_KEY_BLOCKS = ATT_WIN // ATT_Q
ATT_DIAG = ATT_WIN + ATT_Q
LOG2_E = 1.4426950408889634


def _cparams(semantics, vmem_bytes):
    limit = min(int(vmem_bytes), V7X_VMEM_BYTES - 4 * 2**20)
    return pltpu.CompilerParams(dimension_semantics=semantics, vmem_limit_bytes=limit)


def _nbytes(shape, dtype):
    n = 1
    for s in shape:
        n *= s
    return n * jnp.dtype(dtype).itemsize


def _ada_kernel(c_ref, w_ref, b_ref, o_ref):
    c = c_ref[...]
    cond = (c * jax.nn.sigmoid(c)).astype(BF16)
    o_ref[0] = jnp.dot(cond, w_ref[0].astype(BF16), preferred_element_type=F32) + b_ref[0]


def _ada_modulation(c, w_ada, b_ada):
    depth, d, n = w_ada.shape
    bsz = c.shape[0]
    rows = -(-bsz // 8) * 8
    bn = 1024
    c_pad = jnp.pad(c, ((0, rows - bsz), (0, 0)))
    vmem = 2 * (_nbytes((d, bn), F32) + _nbytes((rows, bn), F32)) + _nbytes((d, bn), BF16) + 2**20
    out = pl.pallas_call(
        _ada_kernel,
        grid=(depth, n // bn),
        in_specs=[
            pl.BlockSpec((rows, d), lambda l, j: (0, 0)),
            pl.BlockSpec((1, d, bn), lambda l, j: (l, 0, j)),
            pl.BlockSpec((1, 1, bn), lambda l, j: (l, 0, j)),
        ],
        out_specs=pl.BlockSpec((1, rows, bn), lambda l, j: (l, 0, j)),
        out_shape=jax.ShapeDtypeStruct((depth, rows, n), F32),
        compiler_params=_cparams(("arbitrary", "arbitrary"), vmem),
        name="ada_modulation",
    )(c_pad, w_ada, b_ada.reshape(depth, 1, n))
    return out[:, :bsz]


def _matmul_kernel(a_ref, w_ref, o_ref):
    o_ref[...] = jnp.dot(a_ref[...], w_ref[...], preferred_element_type=F32).astype(o_ref.dtype)


def _matmul(a, w_stack, idx, name):
    m, k = a.shape
    n = w_stack.shape[2]
    bm = min(1024, m)
    bn = min(2048, n)
    vmem = (2 * (_nbytes((bm, k), BF16) + _nbytes((k, bn), BF16) + _nbytes((bm, bn), BF16))
            + _nbytes((bm, bn), F32) + 2**20)
    return pl.pallas_call(
        _matmul_kernel,
        grid=(m // bm, n // bn),
        in_specs=[
            pl.BlockSpec((bm, k), lambda i, j: (i, 0)),
            pl.BlockSpec((None, k, bn), lambda i, j: (idx, 0, j)),
        ],
        out_specs=pl.BlockSpec((bm, bn), lambda i, j: (i, j)),
        out_shape=jax.ShapeDtypeStruct((m, n), BF16),
        compiler_params=_cparams(("arbitrary", "arbitrary"), vmem),
        name=name,
    )(a, w_stack)


def _mix_rows(z, log_a, halo, states, first_row, wpool_ref, pscale_ref, gng_ref, o_ref, out_row):
    rows = z.shape[0]
    n_chunks = rows // CHUNK

    pooled = z[:, 0:POOL_WIDTH].astype(F32)
    ext = jnp.concatenate([halo, pooled], axis=0)
    grow = lax.broadcasted_iota(jnp.int32, (rows, 1), 0) + first_row
    for gi, w in enumerate(POOL_WINDOWS):
        cols = slice(gi * POOL_GROUP, (gi + 1) * POOL_GROUP)
        e = ext[:, cols]
        s = e
        step = 1
        while step < w:
            s = s + pltpu.roll(s, step, 0)
            step *= 2
        count = jnp.minimum(grow + 1, w).astype(F32)
        dev = s[HALO:] / count - e[HALO:]
        y_a = jnp.dot(dev.astype(BF16), wpool_ref[gi], preferred_element_type=F32) * pscale_ref[:, cols]
        o_ref[0, out_row:out_row + rows, cols] = y_a.astype(BF16)

    in_chunk = lax.broadcasted_iota(jnp.int32, (rows, 1), 0) % CHUNK
    b_all = log_a
    step = 1
    while step < CHUNK:
        b_all = b_all + jnp.where(in_chunk >= step, pltpu.roll(b_all, step, 0), 0.0)
        step *= 2
    q_all = z[:, OFF_Q:OFF_K].astype(F32)
    k_all = z[:, OFF_K:OFF_V].astype(F32)
    q_dec = ((q_all * (GLA_DK ** -0.5)) * jnp.exp(b_all)).astype(BF16)
    k_inv = (k_all * jnp.exp(-b_all)).astype(BF16)
    causal = (lax.broadcasted_iota(jnp.int32, (CHUNK, CHUNK), 0)
              >= lax.broadcasted_iota(jnp.int32, (CHUNK, CHUNK), 1))
    nt = (((1,), (1,)), ((), ()))
    units = [(c, h) for c in range(n_chunks) for h in range(GLA_HEADS)]

    def cut(c, h):
        return (slice(c * CHUNK, (c + 1) * CHUNK), slice(h * GLA_DK, (h + 1) * GLA_DK),
                slice(h * GLA_DV, (h + 1) * GLA_DV))

    intra, update, decay = {}, {}, {}
    for c in range(n_chunks):
        rs = slice(c * CHUNK, (c + 1) * CHUNK)
        b_c = b_all[rs]
        b_last = b_c[CHUNK - 1:CHUNK]
        k_dec = (k_all[rs] * jnp.exp(b_last - b_c)).astype(BF16)
        decay_c = jnp.exp(b_last)
        for h in range(GLA_HEADS):
            _, ks, vs = cut(c, h)
            vh = z[rs, OFF_V + h * GLA_DV:OFF_V + (h + 1) * GLA_DV]
            scores = lax.dot_general(q_dec[rs, ks], k_inv[rs, ks], nt, preferred_element_type=F32)
            scores = jnp.where(causal, scores, 0.0).astype(BF16)
            intra[c, h] = jnp.dot(scores, vh, preferred_element_type=F32)
            update[c, h] = lax.dot_general(vh, k_dec[:, ks], (((0,), (0,)), ((), ())),
                                           preferred_element_type=F32)
            decay[c, h] = decay_c[:, ks]
    before = {}
    new_states = []
    for h in range(GLA_HEADS):
        state = states[h]
        for c in range(n_chunks):
            before[c, h] = state.astype(BF16)
            state = decay[c, h] * state + update[c, h]
        new_states.append(state)
    for c, h in units:
        rs, ks, vs = cut(c, h)
        o = intra[c, h] + lax.dot_general(q_dec[rs, ks], before[c, h], nt, preferred_element_type=F32)
        mu = jnp.mean(o, axis=-1, keepdims=True)
        oc = o - mu
        var = jnp.mean(oc * oc, axis=-1, keepdims=True)
        o_n = (oc * lax.rsqrt(var + LN_EPS)) * gng_ref[:, vs]
        r_h = z[rs, OFF_R + h * GLA_DV:OFF_R + (h + 1) * GLA_DV].astype(F32)
        y_b = (r_h * jax.nn.sigmoid(r_h)) * o_n
        o_ref[0, out_row + c * CHUNK:out_row + (c + 1) * CHUNK,
              POOL_WIDTH + h * GLA_DV:POOL_WIDTH + (h + 1) * GLA_DV] = y_b.astype(BF16)
    return pooled[rows - HALO:rows], new_states


def _even_front_kernel(*refs, sub, modulate):
    if modulate:
        sc_ref, sh_ref = refs[1:3]
        refs = refs[:1] + refs[3:]
    u_ref, w_ref, wlr_ref, wg_ref, bg_ref, wpool_ref, pscale_ref, gng_ref, o_ref, state_ref, halo_ref = refs
    t = pl.program_id(1)
    rows = u_ref.shape[1]

    @pl.when(t == 0)
    def _():
        state_ref[...] = jnp.zeros(state_ref.shape, F32)
        halo_ref[...] = jnp.zeros(halo_ref.shape, F32)

    states = [state_ref[h] for h in range(GLA_HEADS)]
    halo = halo_ref[...]
    for r in range(rows // sub):
        u = u_ref[0, r * sub:(r + 1) * sub, :]
        if modulate:
            u = (u * (1.0 + sc_ref[0]) + sh_ref[0]).astype(BF16)
        z = jnp.dot(u, w_ref[...], preferred_element_type=F32).astype(BF16)
        z_lr = jnp.dot(u, wlr_ref[...], preferred_element_type=F32)
        lr_col = lax.broadcasted_iota(jnp.int32, z_lr.shape, 1)
        z_lr = jnp.where(lr_col < GLA_GATE_RANK, z_lr, 0.0).astype(BF16)
        pre = jnp.dot(z_lr, wg_ref[...], preferred_element_type=F32) + bg_ref[...]
        log_a = (jnp.minimum(pre, 0.0) - jnp.log1p(jnp.exp(-jnp.abs(pre)))) * (1.0 / GLA_TAU)
        halo, states = _mix_rows(z, log_a, halo, states, t * rows + r * sub, wpool_ref, pscale_ref, gng_ref,
                                 o_ref, r * sub)
    for h in range(GLA_HEADS):
        state_ref[h] = states[h]
    halo_ref[...] = halo


def _even_front(u, e, w_in, w_gate, b_gate, w_pool, pool_scale, gla_norm_g, modulation=None):
    bsz, seq, d = u.shape
    hk = w_gate.shape[2]
    rows = min(1024, seq)
    sub = min(256, rows)

    def layer(shape, *tail):
        return pl.BlockSpec((None,) + shape, lambda b, t: (e,) + (tail or (0,) * len(shape)))

    vec = pl.BlockSpec((1, 1, d), lambda b, t: (b, 0, 0))
    lead_specs = [vec, vec] if modulation else []
    vmem = (_nbytes((d, OFF_LR), BF16) + 2 * (_nbytes((rows, d), u.dtype) + _nbytes((rows, D_MODEL), BF16))
            + 2 * _nbytes((sub, OFF_LR), F32) + 24 * _nbytes((sub, POOL_WIDTH), F32) + 4 * 2**20)
    return pl.pallas_call(
        functools.partial(_even_front_kernel, sub=sub, modulate=bool(modulation)),
        grid=(bsz, seq // rows),
        in_specs=[pl.BlockSpec((1, rows, d), lambda b, t: (b, t, 0))] + lead_specs + [
            pl.BlockSpec((None, d, OFF_LR), lambda b, t: (e, 0, 0), pipeline_mode=pl.Buffered(1)),
            layer((d, LANES), 0, OFF_LR // LANES),
            layer((LANES, hk)), layer((1, hk)),
            layer((len(POOL_WINDOWS), POOL_GROUP, POOL_GROUP)), layer((1, POOL_WIDTH)),
            layer((1, GLA_HEADS * GLA_DV)),
        ],
        out_specs=pl.BlockSpec((1, rows, D_MODEL), lambda b, t: (b, t, 0)),
        out_shape=jax.ShapeDtypeStruct((bsz, seq, D_MODEL), BF16),
        scratch_shapes=[pltpu.VMEM((GLA_HEADS, GLA_DV, GLA_DK), F32), pltpu.VMEM((HALO, POOL_WIDTH), F32)],
        compiler_params=_cparams(("arbitrary", "arbitrary"), vmem),
        name="even_front",
    )(u, *(modulation or ()), w_in, w_in, w_gate, b_gate, w_pool, pool_scale, gla_norm_g)


def _attn_kernel(diag_ref, q_ref, k0_ref, k1_ref, k2_ref, v0_ref, v1_ref, v2_ref, o_ref, bias_ref):
    hb = diag_ref.shape[0]

    @pl.when((pl.program_id(1) == 0) & (pl.program_id(2) == 0))
    def _():
        qpos = lax.broadcasted_iota(jnp.int32, (ATT_Q, ATT_WIN), 0)
        kpos = lax.broadcasted_iota(jnp.int32, (ATT_Q, ATT_WIN), 1)
        lag = kpos // CHUNK - qpos // CHUNK
        in_band = (lag >= 0) & (lag <= ATT_PAST_CHUNKS)
        for h in range(hb):
            rows = jnp.broadcast_to(diag_ref[h], (ATT_Q, ATT_DIAG))
            table = pltpu.roll(rows, 0, 1, stride=1, stride_axis=0)[:, :ATT_WIN] * LOG2_E
            table = jnp.where(in_band, table, NEG_INF)
            for step in range(ATT_KEY_BLOCKS):
                lead = (ATT_KEY_BLOCKS - 1 - step) * ATT_Q
                bias_ref[step, h] = jnp.where(kpos >= lead, table, NEG_INF)

    variant = jnp.minimum(pl.program_id(2), ATT_KEY_BLOCKS - 1)
    k_refs = (k0_ref, k1_ref, k2_ref)
    v_refs = (v0_ref, v1_ref, v2_ref)
    nt = (((1,), (1,)), ((), ()))
    def head_cols(h):
        return slice(h * ATT_HEAD_DIM, (h + 1) * ATT_HEAD_DIM)

    def qk(h):
        q = q_ref[0, :, head_cols(h)]
        return jnp.concatenate(
            [lax.dot_general(q, kr[0, :, head_cols(h)], nt, preferred_element_type=F32) for kr in k_refs],
            axis=1)

    def softmax(h, s):
        s = s + bias_ref[variant, h]
        p = jnp.exp2(s - jnp.max(s, axis=-1, keepdims=True))
        return p.astype(BF16), jnp.sum(p, axis=-1, keepdims=True)

    def pv(h, p, denom):
        o = jnp.dot(p[:, 0:ATT_Q], v_refs[0][0, :, head_cols(h)], preferred_element_type=F32)
        for j in range(1, ATT_KEY_BLOCKS):
            o = o + jnp.dot(p[:, j * ATT_Q:(j + 1) * ATT_Q], v_refs[j][0, :, head_cols(h)],
                            preferred_element_type=F32)
        o_ref[0, :, head_cols(h)] = (o / denom).astype(BF16)

    scores, probs = {}, {}
    for t in range(hb + 2):
        if t < hb:
            scores[t] = qk(t)
        if 0 <= t - 1 < hb:
            probs[t - 1] = softmax(t - 1, scores.pop(t - 1))
        if 0 <= t - 2 < hb:
            pv(t - 2, *probs.pop(t - 2))


def _band_attention(qkv, diag):
    bsz, seq, _ = qkv.shape
    hb = ATT_HB
    width = hb * ATT_HEAD_DIM
    groups = ATT_HEADS // hb
    blk = (1, ATT_Q, width)

    def key_spec(col0, back):
        return pl.BlockSpec(blk, lambda g, b, i: (b, jnp.maximum(i - back, 0), col0 + g))

    vmem = (2 * 8 * _nbytes(blk, BF16) + _nbytes((ATT_KEY_BLOCKS, hb, ATT_Q, ATT_WIN), F32)
            + 8 * _nbytes((ATT_Q, ATT_WIN), F32) + 4 * 2**20)
    return pl.pallas_call(
        _attn_kernel,
        grid=(groups, bsz, seq // ATT_Q),
        in_specs=[
            pl.BlockSpec((hb, 1, ATT_DIAG), lambda g, b, i: (g, 0, 0)),
            pl.BlockSpec(blk, lambda g, b, i: (b, i, g)),
            key_spec(groups, 2), key_spec(groups, 1), key_spec(groups, 0),
            key_spec(2 * groups, 2), key_spec(2 * groups, 1), key_spec(2 * groups, 0),
        ],
        out_specs=pl.BlockSpec(blk, lambda g, b, i: (b, i, g)),
        out_shape=jax.ShapeDtypeStruct((bsz, seq, D_MODEL), BF16),
        scratch_shapes=[pltpu.VMEM((ATT_KEY_BLOCKS, hb, ATT_Q, ATT_WIN), F32)],
        compiler_params=_cparams(("arbitrary", "arbitrary", "arbitrary"), vmem),
        name="band_attention",
    )(diag, qkv, qkv, qkv, qkv, qkv, qkv, qkv)


def _attention_bias_diagonal(rel_bias):
    pad = ATT_PAST_CHUNKS * CHUNK
    m = jnp.arange(ATT_DIAG)
    q_minus_k = jnp.where(m < ATT_WIN, -m, ATT_DIAG - m)
    rel = jnp.clip(pad + q_minus_k, -REL_CLIP, REL_CLIP) + REL_CLIP
    return rel_bias[:, rel].astype(F32)[:, None, :]


def _mm_ln_kernel(a_ref, w_ref, x_ref, gate_ref, lng_ref, lnb_ref, *rest, alpha, emit_u, sub):
    if emit_u:
        sc_ref, sh_ref, xo_ref, uo_ref = rest
    else:
        (xo_ref,) = rest
    for r in range(a_ref.shape[1] // sub):
        rows = slice(r * sub, (r + 1) * sub)
        y = jnp.dot(a_ref[0, rows, :], w_ref[...], preferred_element_type=F32)
        t = alpha * x_ref[0, rows, :] + (1.0 + gate_ref[0]) * y
        mu = jnp.mean(t, axis=-1, keepdims=True)
        tc = t - mu
        var = jnp.mean(tc * tc, axis=-1, keepdims=True)
        xn = (tc * lax.rsqrt(var + LN_EPS)) * lng_ref[...] + lnb_ref[...]
        xo_ref[0, rows, :] = xn
        if emit_u:
            uo_ref[0, rows, :] = (xn * (1.0 + sc_ref[0]) + sh_ref[0]).astype(BF16)


def _mm_ln(a, w_stack, idx, x, gate, ln_g, ln_b, next_sc, next_sh, alpha, bm, name):
    bsz, seq, kdim = a.shape
    d = w_stack.shape[2]
    bm = min(bm, seq)
    sub = min(256, bm)
    emit_u = next_sc is not None
    vec = pl.BlockSpec((1, 1, d), lambda b, i: (b, 0, 0))
    par = pl.BlockSpec((1, d), lambda b, i: (0, 0))
    row = pl.BlockSpec((1, bm, d), lambda b, i: (b, i, 0))
    in_specs = [
        pl.BlockSpec((1, bm, kdim), lambda b, i: (b, i, 0)),
        pl.BlockSpec((None, kdim, d), lambda b, i: (idx, 0, 0), pipeline_mode=pl.Buffered(1)),
        row, vec, par, par,
    ]
    args = [a, w_stack, x, gate, ln_g, ln_b]
    out_specs = [row]
    out_shape = [jax.ShapeDtypeStruct((bsz, seq, d), F32)]
    if emit_u:
        in_specs += [vec, vec]
        args += [next_sc, next_sh]
        out_specs.append(row)
        out_shape.append(jax.ShapeDtypeStruct((bsz, seq, d), BF16))
    vmem = (_nbytes((kdim, d), BF16)
            + 2 * (_nbytes((bm, kdim), BF16) + 2 * _nbytes((bm, d), F32) + _nbytes((bm, d), BF16))
            + 6 * _nbytes((sub, d), F32) + 2 * 2**20)
    outs = pl.pallas_call(
        functools.partial(_mm_ln_kernel, alpha=alpha, emit_u=emit_u, sub=sub),
        grid=(bsz, seq // bm),
        in_specs=in_specs,
        out_specs=out_specs,
        out_shape=out_shape,
        compiler_params=_cparams(("arbitrary", "arbitrary"), vmem),
        name=name,
    )(*args)
    return (outs[0], outs[1]) if emit_u else (outs[0], None)


def _ffn_up_kernel(u_ref, wa_ref, wg_ref, cw_ref, cb_ref, h_ref, tail_ref, *, sub):
    @pl.when(pl.program_id(2) == 0)
    def _():
        tail_ref[...] = jnp.zeros(tail_ref.shape, F32)

    tail = tail_ref[...]
    row = lax.broadcasted_iota(jnp.int32, (sub, 1), 0)
    for r in range(u_ref.shape[1] // sub):
        rows = slice(r * sub, (r + 1) * sub)
        u = u_ref[0, rows, :]
        prev1 = tail[CONV_TAIL - 1:CONV_TAIL]
        prev2 = tail[CONV_TAIL - 2:CONV_TAIL - 1]
        g = jnp.dot(u, wg_ref[...], preferred_element_type=F32)
        g1 = jnp.where(row == 0, prev1, pltpu.roll(g, 1, 0))
        g2 = jnp.where(row == 0, prev2, jnp.where(row == 1, prev1, pltpu.roll(g, 2, 0)))
        gate = jax.nn.gelu(cb_ref[...] + g2 * cw_ref[0:1] + g1 * cw_ref[1:2] + g * cw_ref[2:3])
        a = jnp.dot(u, wa_ref[...], preferred_element_type=F32)
        h_ref[0, rows, :] = (gate * a).astype(BF16)
        tail = g[sub - CONV_TAIL:sub]
    tail_ref[...] = tail


def _ffn_up(u, layer, w_a, w_g, conv_w, conv_b):
    bsz, seq, d = u.shape
    ffn = w_g.shape[2]
    bm = min(2048, seq)
    sub = min(1024, bm)
    bn = FFN_TILE
    n_tiles = -(-ffn // bn)
    vmem = (2 * (_nbytes((bm, d), BF16) + 2 * _nbytes((d, bn), BF16) + _nbytes((bm, bn), BF16))
            + 8 * _nbytes((sub, bn), F32) + 2 * 2**20)
    return pl.pallas_call(
        functools.partial(_ffn_up_kernel, sub=sub),
        grid=(bsz, n_tiles, seq // bm),
        in_specs=[
            pl.BlockSpec((1, bm, d), lambda b, j, i: (b, i, 0)),
            pl.BlockSpec((None, d, bn), lambda b, j, i: (layer, 0, j)),
            pl.BlockSpec((None, d, bn), lambda b, j, i: (layer, 0, j)),
            pl.BlockSpec((None, 3, bn), lambda b, j, i: (layer, 0, j)),
            pl.BlockSpec((None, 1, bn), lambda b, j, i: (layer, 0, j)),
        ],
        out_specs=pl.BlockSpec((1, bm, bn), lambda b, j, i: (b, i, j)),
        out_shape=jax.ShapeDtypeStruct((bsz, seq, ffn), BF16),
        scratch_shapes=[pltpu.VMEM((CONV_TAIL, bn), F32)],
        compiler_params=_cparams(("arbitrary", "arbitrary", "arbitrary"), vmem),
        name="ffn_up",
    )(u, w_a, w_g, conv_w, conv_b)


def kernel(x, c, w_ada, b_ada, ln_g, ln_b, w_in_ab, w_gate_lr, b_gate, gla_norm_g, w_pool, pool_scale,
           w_out_ab, w_qkv, rel_bias, w_o, w_up, conv_w, conv_b, w_down):
    bsz, seq, d = x.shape
    depth = w_ada.shape[0]
    alpha = (2.0 * depth) ** 0.25

    mod = _ada_modulation(c, w_ada, b_ada).reshape(depth, bsz, 6, 1, d)

    def mod_vec(layer, idx):
        return mod[layer, :, idx]

    w_in = w_in_ab.astype(BF16)
    w_gate = jnp.pad(w_gate_lr, ((0, 0), (0, LANES - GLA_GATE_RANK), (0, 0))).astype(BF16)
    w_pool_b = w_pool.astype(BF16)
    w_out = w_out_ab.astype(BF16)
    col_scale = jnp.where(jnp.arange(3 * d) < d, ATT_HEAD_DIM ** -0.5 * LOG2_E, 1.0).astype(F32)
    w_att = (w_qkv * col_scale[None, None, :]).astype(BF16)
    w_o_b = w_o.astype(BF16)
    w_up_b = w_up.astype(BF16)
    w_g = w_up_b[:, :, FFN_DIM:]
    w_down_b = w_down.astype(BF16)

    u = None
    for layer in range(depth):
        half = layer // 2
        if layer % 2 == 0:
            first = (x, (mod_vec(0, 1), mod_vec(0, 0))) if layer == 0 else (u, None)
            y = _even_front(first[0], half, w_in, w_gate, b_gate[:, None, :], w_pool_b,
                            pool_scale[:, None, :], gla_norm_g[:, None, :], modulation=first[1])
            w_proj = w_out
        else:
            qkv = _matmul(u.reshape(bsz * seq, d), w_att, half, "qkv_proj").reshape(bsz, seq, 3 * d)
            y = _band_attention(qkv, _attention_bias_diagonal(rel_bias[half]))
            w_proj = w_o_b
        x, u = _mm_ln(y, w_proj, half, x, mod_vec(layer, 2), ln_g[layer, 0][None], ln_b[layer, 0][None],
                      mod_vec(layer, 4), mod_vec(layer, 3), alpha, 512, "mixer_out_ln")
        h = _ffn_up(u, layer, w_up_b, w_g, conv_w, conv_b[:, None, :])
        last = layer == depth - 1
        x, u = _mm_ln(h, w_down_b, layer, x, mod_vec(layer, 5), ln_g[layer, 1][None], ln_b[layer, 1][None],
                      None if last else mod_vec(layer + 1, 1), None if last else mod_vec(layer + 1, 0),
                      alpha, 512, "ffn_down_ln")
    return x
```

```python
import functools

import jax
import jax.numpy as jnp
from jax import lax
from jax.experimental import pallas as pl
from jax.experimental.pallas import tpu as pltpu

F32 = jnp.float32
BF16 = jnp.bfloat16

D_MODEL = 2048
CHUNK = 64
POOL_WIDTH = D_MODEL // 2
POOL_WINDOWS = (2, 4, 8, 16)
POOL_GROUP = POOL_WIDTH // len(POOL_WINDOWS)
GLA_HEADS = 4
GLA_DV = (D_MODEL // 2) // GLA_HEADS
GLA_DK = GLA_DV // 2
GLA_GATE_RANK = 16
GLA_TAU = 16.0
OFF_Q = POOL_WIDTH
OFF_K = OFF_Q + GLA_HEADS * GLA_DK
OFF_V = OFF_K + GLA_HEADS * GLA_DK
OFF_R = OFF_V + GLA_HEADS * GLA_DV
OFF_LR = OFF_R + GLA_HEADS * GLA_DV
ATT_HEADS = 16
ATT_HEAD_DIM = D_MODEL // ATT_HEADS
ATT_PAST_CHUNKS = 8
REL_CLIP = 256
FFN_DIM = 5504
LN_EPS = 1e-5
NEG_INF = -1e30

V7X_VMEM_BYTES = 64 * 2**20
LANES = 128
F32_SUBLANES = 8
BF16_SUBLANES = 16
CONV_TAIL = F32_SUBLANES

FFN_TILE = 512
HALO = BF16_SUBLANES
ATT_Q = 4 * CHUNK
ATT_WIN = ATT_Q + ATT_PAST_CHUNKS * CHUNK
ATT_HB = 8
ATT_KEY_BLOCKS = ATT_WIN // ATT_Q
ATT_DIAG = ATT_WIN + ATT_Q
ATT_SPAN = (ATT_PAST_CHUNKS + ATT_Q // (2 * CHUNK)) * CHUNK
LOG2_E = 1.4426950408889634


def _cparams(semantics, vmem_bytes):
    limit = min(int(vmem_bytes), V7X_VMEM_BYTES - 4 * 2**20)
    return pltpu.CompilerParams(dimension_semantics=semantics, vmem_limit_bytes=limit)


def _nbytes(shape, dtype):
    n = 1
    for s in shape:
        n *= s
    return n * jnp.dtype(dtype).itemsize


def _ada_kernel(c_ref, w_ref, b_ref, o_ref):
    c = c_ref[...]
    cond = (c * jax.nn.sigmoid(c)).astype(BF16)
    o_ref[0] = jnp.dot(cond, w_ref[0].astype(BF16), preferred_element_type=F32) + b_ref[0]


def _ada_modulation(c, w_ada, b_ada):
    depth, d, n = w_ada.shape
    bsz = c.shape[0]
    rows = -(-bsz // 8) * 8
    bn = 1024
    c_pad = jnp.pad(c, ((0, rows - bsz), (0, 0)))
    vmem = 2 * (_nbytes((d, bn), F32) + _nbytes((rows, bn), F32)) + _nbytes((d, bn), BF16) + 2**20
    out = pl.pallas_call(
        _ada_kernel,
        grid=(depth, n // bn),
        in_specs=[
            pl.BlockSpec((rows, d), lambda l, j: (0, 0)),
            pl.BlockSpec((1, d, bn), lambda l, j: (l, 0, j)),
            pl.BlockSpec((1, 1, bn), lambda l, j: (l, 0, j)),
        ],
        out_specs=pl.BlockSpec((1, rows, bn), lambda l, j: (l, 0, j)),
        out_shape=jax.ShapeDtypeStruct((depth, rows, n), F32),
        compiler_params=_cparams(("arbitrary", "arbitrary"), vmem),
        name="ada_modulation",
    )(c_pad, w_ada, b_ada.reshape(depth, 1, n))
    return out[:, :bsz]


def _matmul_kernel(a_ref, w_ref, o_ref):
    o_ref[...] = jnp.dot(a_ref[...], w_ref[...], preferred_element_type=F32).astype(o_ref.dtype)


def _matmul(a, w_stack, idx, name):
    m, k = a.shape
    n = w_stack.shape[2]
    bm = min(1024, m)
    bn = min(2048, n)
    vmem = (2 * (_nbytes((bm, k), BF16) + _nbytes((k, bn), BF16) + _nbytes((bm, bn), BF16))
            + _nbytes((bm, bn), F32) + 2**20)
    return pl.pallas_call(
        _matmul_kernel,
        grid=(m // bm, n // bn),
        in_specs=[
            pl.BlockSpec((bm, k), lambda i, j: (i, 0)),
            pl.BlockSpec((None, k, bn), lambda i, j: (idx, 0, j)),
        ],
        out_specs=pl.BlockSpec((bm, bn), lambda i, j: (i, j)),
        out_shape=jax.ShapeDtypeStruct((m, n), BF16),
        compiler_params=_cparams(("arbitrary", "arbitrary"), vmem),
        name=name,
    )(a, w_stack)


def _mix_rows(z, log_a, halo, states, first_row, wpool_ref, pscale_ref, gng_ref, o_ref, out_row):
    rows = z.shape[0]
    n_chunks = rows // CHUNK

    pooled = z[:, 0:POOL_WIDTH].astype(F32)
    ext = jnp.concatenate([halo, pooled], axis=0)
    grow = lax.broadcasted_iota(jnp.int32, (rows, 1), 0) + first_row
    for gi, w in enumerate(POOL_WINDOWS):
        cols = slice(gi * POOL_GROUP, (gi + 1) * POOL_GROUP)
        e = ext[:, cols]
        s = e
        step = 1
        while step < w:
            s = s + pltpu.roll(s, step, 0)
            step *= 2
        count = jnp.minimum(grow + 1, w).astype(F32)
        dev = s[HALO:] / count - e[HALO:]
        y_a = jnp.dot(dev.astype(BF16), wpool_ref[gi], preferred_element_type=F32) * pscale_ref[:, cols]
        o_ref[0, out_row:out_row + rows, cols] = y_a.astype(BF16)

    in_chunk = lax.broadcasted_iota(jnp.int32, (rows, 1), 0) % CHUNK
    b_all = log_a
    step = 1
    while step < CHUNK:
        b_all = b_all + jnp.where(in_chunk >= step, pltpu.roll(b_all, step, 0), 0.0)
        step *= 2
    q_all = z[:, OFF_Q:OFF_K].astype(F32)
    k_all = z[:, OFF_K:OFF_V].astype(F32)
    q_dec = ((q_all * (GLA_DK ** -0.5)) * jnp.exp(b_all)).astype(BF16)
    k_inv = (k_all * jnp.exp(-b_all)).astype(BF16)
    causal = (lax.broadcasted_iota(jnp.int32, (CHUNK, CHUNK), 0)
              >= lax.broadcasted_iota(jnp.int32, (CHUNK, CHUNK), 1))
    nt = (((1,), (1,)), ((), ()))
    units = [(c, h) for c in range(n_chunks) for h in range(GLA_HEADS)]

    def cut(c, h):
        return (slice(c * CHUNK, (c + 1) * CHUNK), slice(h * GLA_DK, (h + 1) * GLA_DK),
                slice(h * GLA_DV, (h + 1) * GLA_DV))

    intra, update, decay = {}, {}, {}
    for c in range(n_chunks):
        rs = slice(c * CHUNK, (c + 1) * CHUNK)
        b_c = b_all[rs]
        b_last = b_c[CHUNK - 1:CHUNK]
        k_dec = (k_all[rs] * jnp.exp(b_last - b_c)).astype(BF16)
        decay_c = jnp.exp(b_last)
        for h in range(GLA_HEADS):
            _, ks, vs = cut(c, h)
            vh = z[rs, OFF_V + h * GLA_DV:OFF_V + (h + 1) * GLA_DV]
            scores = lax.dot_general(q_dec[rs, ks], k_inv[rs, ks], nt, preferred_element_type=F32)
            scores = jnp.where(causal, scores, 0.0).astype(BF16)
            intra[c, h] = jnp.dot(scores, vh, preferred_element_type=F32)
            update[c, h] = lax.dot_general(vh, k_dec[:, ks], (((0,), (0,)), ((), ())),
                                           preferred_element_type=F32)
            decay[c, h] = decay_c[:, ks]
    before = {}
    new_states = []
    for h in range(GLA_HEADS):
        state = states[h]
        for c in range(n_chunks):
            before[c, h] = state.astype(BF16)
            state = decay[c, h] * state + update[c, h]
        new_states.append(state)
    for c, h in units:
        rs, ks, vs = cut(c, h)
        o = intra[c, h] + lax.dot_general(q_dec[rs, ks], before[c, h], nt, preferred_element_type=F32)
        mu = jnp.mean(o, axis=-1, keepdims=True)
        oc = o - mu
        var = jnp.mean(oc * oc, axis=-1, keepdims=True)
        o_n = (oc * lax.rsqrt(var + LN_EPS)) * gng_ref[:, vs]
        r_h = z[rs, OFF_R + h * GLA_DV:OFF_R + (h + 1) * GLA_DV].astype(F32)
        y_b = (r_h * jax.nn.sigmoid(r_h)) * o_n
        o_ref[0, out_row + c * CHUNK:out_row + (c + 1) * CHUNK,
              POOL_WIDTH + h * GLA_DV:POOL_WIDTH + (h + 1) * GLA_DV] = y_b.astype(BF16)
    return pooled[rows - HALO:rows], new_states


def _even_front_kernel(*refs, sub, modulate):
    if modulate:
        sc_ref, sh_ref = refs[1:3]
        refs = refs[:1] + refs[3:]
    u_ref, w_ref, wlr_ref, wg_ref, bg_ref, wpool_ref, pscale_ref, gng_ref, o_ref, state_ref, halo_ref = refs
    t = pl.program_id(1)
    rows = u_ref.shape[1]

    @pl.when(t == 0)
    def _():
        state_ref[...] = jnp.zeros(state_ref.shape, F32)
        halo_ref[...] = jnp.zeros(halo_ref.shape, F32)

    states = [state_ref[h] for h in range(GLA_HEADS)]
    halo = halo_ref[...]
    for r in range(rows // sub):
        u = u_ref[0, r * sub:(r + 1) * sub, :]
        if modulate:
            u = (u * (1.0 + sc_ref[0]) + sh_ref[0]).astype(BF16)
        z = jnp.dot(u, w_ref[...], preferred_element_type=F32).astype(BF16)
        z_lr = jnp.dot(u, wlr_ref[...], preferred_element_type=F32)
        lr_col = lax.broadcasted_iota(jnp.int32, z_lr.shape, 1)
        z_lr = jnp.where(lr_col < GLA_GATE_RANK, z_lr, 0.0).astype(BF16)
        pre = jnp.dot(z_lr, wg_ref[...], preferred_element_type=F32) + bg_ref[...]
        log_a = (jnp.minimum(pre, 0.0) - jnp.log1p(jnp.exp(-jnp.abs(pre)))) * (1.0 / GLA_TAU)
        halo, states = _mix_rows(z, log_a, halo, states, t * rows + r * sub, wpool_ref, pscale_ref, gng_ref,
                                 o_ref, r * sub)
    for h in range(GLA_HEADS):
        state_ref[h] = states[h]
    halo_ref[...] = halo


def _even_front(u, e, w_in, w_gate, b_gate, w_pool, pool_scale, gla_norm_g, modulation=None):
    bsz, seq, d = u.shape
    hk = w_gate.shape[2]
    rows = min(1024, seq)
    sub = min(256, rows)

    def layer(shape, *tail):
        return pl.BlockSpec((None,) + shape, lambda b, t: (e,) + (tail or (0,) * len(shape)))

    vec = pl.BlockSpec((1, 1, d), lambda b, t: (b, 0, 0))
    lead_specs = [vec, vec] if modulation else []
    vmem = (_nbytes((d, OFF_LR), BF16) + 2 * (_nbytes((rows, d), u.dtype) + _nbytes((rows, D_MODEL), BF16))
            + 2 * _nbytes((sub, OFF_LR), F32) + 24 * _nbytes((sub, POOL_WIDTH), F32) + 4 * 2**20)
    return pl.pallas_call(
        functools.partial(_even_front_kernel, sub=sub, modulate=bool(modulation)),
        grid=(bsz, seq // rows),
        in_specs=[pl.BlockSpec((1, rows, d), lambda b, t: (b, t, 0))] + lead_specs + [
            pl.BlockSpec((None, d, OFF_LR), lambda b, t: (e, 0, 0), pipeline_mode=pl.Buffered(1)),
            layer((d, LANES), 0, OFF_LR // LANES),
            layer((LANES, hk)), layer((1, hk)),
            layer((len(POOL_WINDOWS), POOL_GROUP, POOL_GROUP)), layer((1, POOL_WIDTH)),
            layer((1, GLA_HEADS * GLA_DV)),
        ],
        out_specs=pl.BlockSpec((1, rows, D_MODEL), lambda b, t: (b, t, 0)),
        out_shape=jax.ShapeDtypeStruct((bsz, seq, D_MODEL), BF16),
        scratch_shapes=[pltpu.VMEM((GLA_HEADS, GLA_DV, GLA_DK), F32), pltpu.VMEM((HALO, POOL_WIDTH), F32)],
        compiler_params=_cparams(("arbitrary", "arbitrary"), vmem),
        name="even_front",
    )(u, *(modulation or ()), w_in, w_in, w_gate, b_gate, w_pool, pool_scale, gla_norm_g)


def _attn_kernel(diag_ref, q_ref, k0_ref, k1_ref, k2_ref, v0_ref, v1_ref, v2_ref, o_ref, bias_ref):
    hb = diag_ref.shape[0]

    @pl.when((pl.program_id(1) == 0) & (pl.program_id(2) == 0))
    def _():
        qpos = lax.broadcasted_iota(jnp.int32, (ATT_Q, ATT_WIN), 0)
        kpos = lax.broadcasted_iota(jnp.int32, (ATT_Q, ATT_WIN), 1)
        lag = kpos // CHUNK - qpos // CHUNK
        in_band = (lag >= 0) & (lag <= ATT_PAST_CHUNKS)
        for h in range(hb):
            rows = jnp.broadcast_to(diag_ref[h], (ATT_Q, ATT_DIAG))
            table = pltpu.roll(rows, 0, 1, stride=1, stride_axis=0)[:, :ATT_WIN] * LOG2_E
            table = jnp.where(in_band, table, NEG_INF)
            for step in range(ATT_KEY_BLOCKS):
                lead = (ATT_KEY_BLOCKS - 1 - step) * ATT_Q
                bias_ref[step, h] = jnp.where(kpos >= lead, table, NEG_INF)

    variant = jnp.minimum(pl.program_id(2), ATT_KEY_BLOCKS - 1)
    k_refs = (k0_ref, k1_ref, k2_ref)
    v_refs = (v0_ref, v1_ref, v2_ref)
    nt = (((1,), (1,)), ((), ()))
    def head_cols(h):
        return slice(h * ATT_HEAD_DIM, (h + 1) * ATT_HEAD_DIM)

    half = ATT_Q // 2
    units = [(h, r) for h in range(hb) for r in range(2)]

    def span(r):
        lo, hi = r * half, r * half + ATT_SPAN
        return [(j, max(lo - j * ATT_Q, 0), min(hi - j * ATT_Q, ATT_Q)) for j in range(ATT_KEY_BLOCKS)]

    def qk(h, r):
        q = q_ref[0, r * half:(r + 1) * half, head_cols(h)]
        return jnp.concatenate(
            [lax.dot_general(q, k_refs[j][0, a:b, head_cols(h)], nt, preferred_element_type=F32)
             for j, a, b in span(r)], axis=1)

    def softmax(h, r, s):
        s = s + bias_ref[variant, h, r * half:(r + 1) * half, r * half:r * half + ATT_SPAN]
        p = jnp.exp2(s - jnp.max(s, axis=-1, keepdims=True))
        return p.astype(BF16), jnp.sum(p, axis=-1, keepdims=True)

    def pv(h, r, p, denom):
        o, col = None, 0
        for j, a, b in span(r):
            part = jnp.dot(p[:, col:col + b - a], v_refs[j][0, a:b, head_cols(h)], preferred_element_type=F32)
            o = part if o is None else o + part
            col += b - a
        o_ref[0, r * half:(r + 1) * half, head_cols(h)] = (o / denom).astype(BF16)

    scores, probs = {}, {}
    for t in range(len(units) + 2):
        if t < len(units):
            scores[t] = qk(*units[t])
        if 0 <= t - 1 < len(units):
            probs[t - 1] = softmax(*units[t - 1], scores.pop(t - 1))
        if 0 <= t - 2 < len(units):
            pv(*units[t - 2], *probs.pop(t - 2))


def _band_attention(qkv, diag):
    bsz, seq, _ = qkv.shape
    hb = ATT_HB
    width = hb * ATT_HEAD_DIM
    groups = ATT_HEADS // hb
    blk = (1, ATT_Q, width)

    def key_spec(col0, back):
        return pl.BlockSpec(blk, lambda g, b, i: (b, jnp.maximum(i - back, 0), col0 + g))

    vmem = (2 * 8 * _nbytes(blk, BF16) + _nbytes((ATT_KEY_BLOCKS, hb, ATT_Q, ATT_WIN), F32)
            + 8 * _nbytes((ATT_Q, ATT_WIN), F32) + 4 * 2**20)
    return pl.pallas_call(
        _attn_kernel,
        grid=(groups, bsz, seq // ATT_Q),
        in_specs=[
            pl.BlockSpec((hb, 1, ATT_DIAG), lambda g, b, i: (g, 0, 0)),
            pl.BlockSpec(blk, lambda g, b, i: (b, i, g)),
            key_spec(groups, 2), key_spec(groups, 1), key_spec(groups, 0),
            key_spec(2 * groups, 2), key_spec(2 * groups, 1), key_spec(2 * groups, 0),
        ],
        out_specs=pl.BlockSpec(blk, lambda g, b, i: (b, i, g)),
        out_shape=jax.ShapeDtypeStruct((bsz, seq, D_MODEL), BF16),
        scratch_shapes=[pltpu.VMEM((ATT_KEY_BLOCKS, hb, ATT_Q, ATT_WIN), F32)],
        compiler_params=_cparams(("arbitrary", "arbitrary", "arbitrary"), vmem),
        name="band_attention",
    )(diag, qkv, qkv, qkv, qkv, qkv, qkv, qkv)


def _attention_bias_diagonal(rel_bias):
    pad = ATT_PAST_CHUNKS * CHUNK
    m = jnp.arange(ATT_DIAG)
    q_minus_k = jnp.where(m < ATT_WIN, -m, ATT_DIAG - m)
    rel = jnp.clip(pad + q_minus_k, -REL_CLIP, REL_CLIP) + REL_CLIP
    return rel_bias[:, rel].astype(F32)[:, None, :]


def _mm_ln_kernel(a_ref, w_ref, x_ref, gate_ref, lng_ref, lnb_ref, *rest, alpha, emit_u, sub):
    if emit_u:
        sc_ref, sh_ref, xo_ref, uo_ref = rest
    else:
        (xo_ref,) = rest
    for r in range(a_ref.shape[1] // sub):
        rows = slice(r * sub, (r + 1) * sub)
        y = jnp.dot(a_ref[0, rows, :], w_ref[...], preferred_element_type=F32)
        t = alpha * x_ref[0, rows, :] + (1.0 + gate_ref[0]) * y
        mu = jnp.mean(t, axis=-1, keepdims=True)
        tc = t - mu
        var = jnp.mean(tc * tc, axis=-1, keepdims=True)
        xn = (tc * lax.rsqrt(var + LN_EPS)) * lng_ref[...] + lnb_ref[...]
        xo_ref[0, rows, :] = xn
        if emit_u:
            uo_ref[0, rows, :] = (xn * (1.0 + sc_ref[0]) + sh_ref[0]).astype(BF16)


def _mm_ln(a, w_stack, idx, x, gate, ln_g, ln_b, next_sc, next_sh, alpha, bm, name):
    bsz, seq, kdim = a.shape
    d = w_stack.shape[2]
    bm = min(bm, seq)
    sub = min(256, bm)
    emit_u = next_sc is not None
    vec = pl.BlockSpec((1, 1, d), lambda b, i: (b, 0, 0))
    par = pl.BlockSpec((1, d), lambda b, i: (0, 0))
    row = pl.BlockSpec((1, bm, d), lambda b, i: (b, i, 0))
    in_specs = [
        pl.BlockSpec((1, bm, kdim), lambda b, i: (b, i, 0)),
        pl.BlockSpec((None, kdim, d), lambda b, i: (idx, 0, 0), pipeline_mode=pl.Buffered(1)),
        row, vec, par, par,
    ]
    args = [a, w_stack, x, gate, ln_g, ln_b]
    out_specs = [row]
    out_shape = [jax.ShapeDtypeStruct((bsz, seq, d), F32)]
    if emit_u:
        in_specs += [vec, vec]
        args += [next_sc, next_sh]
        out_specs.append(row)
        out_shape.append(jax.ShapeDtypeStruct((bsz, seq, d), BF16))
    vmem = (_nbytes((kdim, d), BF16)
            + 2 * (_nbytes((bm, kdim), BF16) + 2 * _nbytes((bm, d), F32) + _nbytes((bm, d), BF16))
            + 6 * _nbytes((sub, d), F32) + 2 * 2**20)
    outs = pl.pallas_call(
        functools.partial(_mm_ln_kernel, alpha=alpha, emit_u=emit_u, sub=sub),
        grid=(bsz, seq // bm),
        in_specs=in_specs,
        out_specs=out_specs,
        out_shape=out_shape,
        compiler_params=_cparams(("arbitrary", "arbitrary"), vmem),
        name=name,
    )(*args)
    return (outs[0], outs[1]) if emit_u else (outs[0], None)


def _ffn_up_kernel(u_ref, wa_ref, wg_ref, cw_ref, cb_ref, h_ref, tail_ref, *, sub):
    @pl.when(pl.program_id(2) == 0)
    def _():
        tail_ref[...] = jnp.zeros(tail_ref.shape, F32)

    tail = tail_ref[...]
    row = lax.broadcasted_iota(jnp.int32, (sub, 1), 0)
    for r in range(u_ref.shape[1] // sub):
        rows = slice(r * sub, (r + 1) * sub)
        u = u_ref[0, rows, :]
        prev1 = tail[CONV_TAIL - 1:CONV_TAIL]
        prev2 = tail[CONV_TAIL - 2:CONV_TAIL - 1]
        g = jnp.dot(u, wg_ref[...], preferred_element_type=F32)
        g1 = jnp.where(row == 0, prev1, pltpu.roll(g, 1, 0))
        g2 = jnp.where(row == 0, prev2, jnp.where(row == 1, prev1, pltpu.roll(g, 2, 0)))
        gate = jax.nn.gelu(cb_ref[...] + g2 * cw_ref[0:1] + g1 * cw_ref[1:2] + g * cw_ref[2:3])
        a = jnp.dot(u, wa_ref[...], preferred_element_type=F32)
        h_ref[0, rows, :] = (gate * a).astype(BF16)
        tail = g[sub - CONV_TAIL:sub]
    tail_ref[...] = tail


def _ffn_up(u, layer, w_a, w_g, conv_w, conv_b):
    bsz, seq, d = u.shape
    ffn = w_g.shape[2]
    bm = min(2048, seq)
    sub = min(1024, bm)
    bn = FFN_TILE
    n_tiles = -(-ffn // bn)
    vmem = (2 * (_nbytes((bm, d), BF16) + 2 * _nbytes((d, bn), BF16) + _nbytes((bm, bn), BF16))
            + 8 * _nbytes((sub, bn), F32) + 2 * 2**20)
    return pl.pallas_call(
        functools.partial(_ffn_up_kernel, sub=sub),
        grid=(bsz, n_tiles, seq // bm),
        in_specs=[
            pl.BlockSpec((1, bm, d), lambda b, j, i: (b, i, 0)),
            pl.BlockSpec((None, d, bn), lambda b, j, i: (layer, 0, j)),
            pl.BlockSpec((None, d, bn), lambda b, j, i: (layer, 0, j)),
            pl.BlockSpec((None, 3, bn), lambda b, j, i: (layer, 0, j)),
            pl.BlockSpec((None, 1, bn), lambda b, j, i: (layer, 0, j)),
        ],
        out_specs=pl.BlockSpec((1, bm, bn), lambda b, j, i: (b, i, j)),
        out_shape=jax.ShapeDtypeStruct((bsz, seq, ffn), BF16),
        scratch_shapes=[pltpu.VMEM((CONV_TAIL, bn), F32)],
        compiler_params=_cparams(("arbitrary", "arbitrary", "arbitrary"), vmem),
        name="ffn_up",
    )(u, w_a, w_g, conv_w, conv_b)


def kernel(x, c, w_ada, b_ada, ln_g, ln_b, w_in_ab, w_gate_lr, b_gate, gla_norm_g, w_pool, pool_scale,
           w_out_ab, w_qkv, rel_bias, w_o, w_up, conv_w, conv_b, w_down):
    bsz, seq, d = x.shape
    depth = w_ada.shape[0]
    alpha = (2.0 * depth) ** 0.25

    mod = _ada_modulation(c, w_ada, b_ada).reshape(depth, bsz, 6, 1, d)

    def mod_vec(layer, idx):
        return mod[layer, :, idx]

    w_in = w_in_ab.astype(BF16)
    w_gate = jnp.pad(w_gate_lr, ((0, 0), (0, LANES - GLA_GATE_RANK), (0, 0))).astype(BF16)
    w_pool_b = w_pool.astype(BF16)
    w_out = w_out_ab.astype(BF16)
    col_scale = jnp.where(jnp.arange(3 * d) < d, ATT_HEAD_DIM ** -0.5 * LOG2_E, 1.0).astype(F32)
    w_att = (w_qkv * col_scale[None, None, :]).astype(BF16)
    w_o_b = w_o.astype(BF16)
    w_up_b = w_up.astype(BF16)
    w_g = w_up_b[:, :, FFN_DIM:]
    w_down_b = w_down.astype(BF16)

    u = None
    for layer in range(depth):
        half = layer // 2
        if layer % 2 == 0:
            first = (x, (mod_vec(0, 1), mod_vec(0, 0))) if layer == 0 else (u, None)
            y = _even_front(first[0], half, w_in, w_gate, b_gate[:, None, :], w_pool_b,
                            pool_scale[:, None, :], gla_norm_g[:, None, :], modulation=first[1])
            w_proj = w_out
        else:
            qkv = _matmul(u.reshape(bsz * seq, d), w_att, half, "qkv_proj").reshape(bsz, seq, 3 * d)
            y = _band_attention(qkv, _attention_bias_diagonal(rel_bias[half]))
            w_proj = w_o_b
        x, u = _mm_ln(y, w_proj, half, x, mod_vec(layer, 2), ln_g[layer, 0][None], ln_b[layer, 0][None],
                      mod_vec(layer, 4), mod_vec(layer, 3), alpha, 512, "mixer_out_ln")
        h = _ffn_up(u, layer, w_up_b, w_g, conv_w, conv_b[:, None, :])
        last = layer == depth - 1
        x, u = _mm_ln(h, w_down_b, layer, x, mod_vec(layer, 5), ln_g[layer, 1][None], ln_b[layer, 1][None],
                      None if last else mod_vec(layer + 1, 1), None if last else mod_vec(layer + 1, 0),
                      alpha, 512, "ffn_down_ln")
    return x
```

```python
import functools

import jax
import jax.numpy as jnp
from jax import lax
from jax.experimental import pallas as pl
from jax.experimental.pallas import tpu as pltpu

F32 = jnp.float32
BF16 = jnp.bfloat16

D_MODEL = 2048
CHUNK = 64
POOL_WIDTH = D_MODEL // 2
POOL_WINDOWS = (2, 4, 8, 16)
POOL_GROUP = POOL_WIDTH // len(POOL_WINDOWS)
GLA_HEADS = 4
GLA_DV = (D_MODEL // 2) // GLA_HEADS
GLA_DK = GLA_DV // 2
GLA_GATE_RANK = 16
GLA_TAU = 16.0
OFF_Q = POOL_WIDTH
OFF_K = OFF_Q + GLA_HEADS * GLA_DK
OFF_V = OFF_K + GLA_HEADS * GLA_DK
OFF_R = OFF_V + GLA_HEADS * GLA_DV
OFF_LR = OFF_R + GLA_HEADS * GLA_DV
ATT_HEADS = 16
ATT_HEAD_DIM = D_MODEL // ATT_HEADS
ATT_PAST_CHUNKS = 8
REL_CLIP = 256
FFN_DIM = 5504
LN_EPS = 1e-5
NEG_INF = -1e30

V7X_VMEM_BYTES = 64 * 2**20
LANES = 128
F32_SUBLANES = 8
BF16_SUBLANES = 16
CONV_TAIL = F32_SUBLANES

FFN_TILE = 512
HALO = BF16_SUBLANES
ATT_Q = 4 * CHUNK
ATT_WIN = ATT_Q + ATT_PAST_CHUNKS * CHUNK
ATT_HB = 8
ATT_KEY_BLOCKS = ATT_WIN // ATT_Q
ATT_DIAG = ATT_WIN + ATT_Q
LOG2_E = 1.4426950408889634


def _cparams(semantics, vmem_bytes, fuse_inputs=None):
    limit = min(int(vmem_bytes), V7X_VMEM_BYTES - 4 * 2**20)
    return pltpu.CompilerParams(dimension_semantics=semantics, vmem_limit_bytes=limit,
                                allow_input_fusion=fuse_inputs)


def _nbytes(shape, dtype):
    n = 1
    for s in shape:
        n *= s
    return n * jnp.dtype(dtype).itemsize


def _ada_kernel(c_ref, w_ref, b_ref, o_ref):
    c = c_ref[...]
    cond = (c * jax.nn.sigmoid(c)).astype(BF16)
    o_ref[0] = jnp.dot(cond, w_ref[0].astype(BF16), preferred_element_type=F32) + b_ref[0]


def _ada_modulation(c, w_ada, b_ada):
    depth, d, n = w_ada.shape
    bsz = c.shape[0]
    rows = -(-bsz // 8) * 8
    bn = 1024
    c_pad = jnp.pad(c, ((0, rows - bsz), (0, 0)))
    vmem = 2 * (_nbytes((d, bn), F32) + _nbytes((rows, bn), F32)) + _nbytes((d, bn), BF16) + 2**20
    out = pl.pallas_call(
        _ada_kernel,
        grid=(depth, n // bn),
        in_specs=[
            pl.BlockSpec((rows, d), lambda l, j: (0, 0)),
            pl.BlockSpec((1, d, bn), lambda l, j: (l, 0, j)),
            pl.BlockSpec((1, 1, bn), lambda l, j: (l, 0, j)),
        ],
        out_specs=pl.BlockSpec((1, rows, bn), lambda l, j: (l, 0, j)),
        out_shape=jax.ShapeDtypeStruct((depth, rows, n), F32),
        compiler_params=_cparams(("arbitrary", "arbitrary"), vmem),
        name="ada_modulation",
    )(c_pad, w_ada, b_ada.reshape(depth, 1, n))
    return out[:, :bsz]


def _matmul_kernel(a_ref, w_ref, o_ref):
    o_ref[...] = jnp.dot(a_ref[...], w_ref[...], preferred_element_type=F32).astype(o_ref.dtype)


def _matmul(a, w_stack, idx, name):
    m, k = a.shape
    n = w_stack.shape[2]
    bm = min(1024, m)
    bn = min(2048, n)
    vmem = (2 * (_nbytes((bm, k), BF16) + _nbytes((k, bn), BF16) + _nbytes((bm, bn), BF16))
            + _nbytes((bm, bn), F32) + 2**20)
    return pl.pallas_call(
        _matmul_kernel,
        grid=(m // bm, n // bn),
        in_specs=[
            pl.BlockSpec((bm, k), lambda i, j: (i, 0)),
            pl.BlockSpec((None, k, bn), lambda i, j: (idx, 0, j)),
        ],
        out_specs=pl.BlockSpec((bm, bn), lambda i, j: (i, j)),
        out_shape=jax.ShapeDtypeStruct((m, n), BF16),
        compiler_params=_cparams(("arbitrary", "arbitrary"), vmem),
        name=name,
    )(a, w_stack)


def _mix_rows(z, log_a, halo, states, first_row, wpool_ref, pscale_ref, gng_ref, o_ref, out_row):
    rows = z.shape[0]
    n_chunks = rows // CHUNK

    pooled = z[:, 0:POOL_WIDTH].astype(F32)
    ext = jnp.concatenate([halo, pooled], axis=0)
    grow = lax.broadcasted_iota(jnp.int32, (rows, 1), 0) + first_row
    for gi, w in enumerate(POOL_WINDOWS):
        cols = slice(gi * POOL_GROUP, (gi + 1) * POOL_GROUP)
        e = ext[:, cols]
        s = e
        step = 1
        while step < w:
            s = s + pltpu.roll(s, step, 0)
            step *= 2
        count = jnp.minimum(grow + 1, w).astype(F32)
        dev = s[HALO:] / count - e[HALO:]
        y_a = jnp.dot(dev.astype(BF16), wpool_ref[gi], preferred_element_type=F32) * pscale_ref[:, cols]
        o_ref[0, out_row:out_row + rows, cols] = y_a.astype(BF16)

    in_chunk = lax.broadcasted_iota(jnp.int32, (rows, 1), 0) % CHUNK
    b_all = log_a
    step = 1
    while step < CHUNK:
        b_all = b_all + jnp.where(in_chunk >= step, pltpu.roll(b_all, step, 0), 0.0)
        step *= 2
    q_all = z[:, OFF_Q:OFF_K].astype(F32)
    k_all = z[:, OFF_K:OFF_V].astype(F32)
    q_dec = ((q_all * (GLA_DK ** -0.5)) * jnp.exp(b_all)).astype(BF16)
    k_inv = (k_all * jnp.exp(-b_all)).astype(BF16)
    causal = (lax.broadcasted_iota(jnp.int32, (CHUNK, CHUNK), 0)
              >= lax.broadcasted_iota(jnp.int32, (CHUNK, CHUNK), 1))
    nt = (((1,), (1,)), ((), ()))
    units = [(c, h) for c in range(n_chunks) for h in range(GLA_HEADS)]

    def cut(c, h):
        return (slice(c * CHUNK, (c + 1) * CHUNK), slice(h * GLA_DK, (h + 1) * GLA_DK),
                slice(h * GLA_DV, (h + 1) * GLA_DV))

    intra, update, decay = {}, {}, {}
    for c in range(n_chunks):
        rs = slice(c * CHUNK, (c + 1) * CHUNK)
        b_c = b_all[rs]
        b_last = b_c[CHUNK - 1:CHUNK]
        k_dec = (k_all[rs] * jnp.exp(b_last - b_c)).astype(BF16)
        decay_c = jnp.exp(b_last)
        for h in range(GLA_HEADS):
            _, ks, vs = cut(c, h)
            vh = z[rs, OFF_V + h * GLA_DV:OFF_V + (h + 1) * GLA_DV]
            scores = lax.dot_general(q_dec[rs, ks], k_inv[rs, ks], nt, preferred_element_type=F32)
            scores = jnp.where(causal, scores, 0.0).astype(BF16)
            intra[c, h] = jnp.dot(scores, vh, preferred_element_type=F32)
            update[c, h] = lax.dot_general(vh, k_dec[:, ks], (((0,), (0,)), ((), ())),
                                           preferred_element_type=F32)
            decay[c, h] = decay_c[:, ks]
    before = {}
    new_states = []
    for h in range(GLA_HEADS):
        state = states[h]
        for c in range(n_chunks):
            before[c, h] = state.astype(BF16)
            state = decay[c, h] * state + update[c, h]
        new_states.append(state)
    for c, h in units:
        rs, ks, vs = cut(c, h)
        o = intra[c, h] + lax.dot_general(q_dec[rs, ks], before[c, h], nt, preferred_element_type=F32)
        mu = jnp.mean(o, axis=-1, keepdims=True)
        oc = o - mu
        var = jnp.mean(oc * oc, axis=-1, keepdims=True)
        o_n = (oc * lax.rsqrt(var + LN_EPS)) * gng_ref[:, vs]
        r_h = z[rs, OFF_R + h * GLA_DV:OFF_R + (h + 1) * GLA_DV].astype(F32)
        y_b = (r_h * jax.nn.sigmoid(r_h)) * o_n
        o_ref[0, out_row + c * CHUNK:out_row + (c + 1) * CHUNK,
              POOL_WIDTH + h * GLA_DV:POOL_WIDTH + (h + 1) * GLA_DV] = y_b.astype(BF16)
    return pooled[rows - HALO:rows], new_states


def _even_front_kernel(*refs, sub, modulate):
    if modulate:
        sc_ref, sh_ref = refs[1:3]
        refs = refs[:1] + refs[3:]
    u_ref, w_ref, wlr_ref, wg_ref, bg_ref, wpool_ref, pscale_ref, gng_ref, o_ref, state_ref, halo_ref = refs
    t = pl.program_id(1)
    rows = u_ref.shape[1]

    @pl.when(t == 0)
    def _():
        state_ref[...] = jnp.zeros(state_ref.shape, F32)
        halo_ref[...] = jnp.zeros(halo_ref.shape, F32)

    states = [state_ref[h] for h in range(GLA_HEADS)]
    halo = halo_ref[...]
    for r in range(rows // sub):
        u = u_ref[0, r * sub:(r + 1) * sub, :]
        if modulate:
            u = (u * (1.0 + sc_ref[0]) + sh_ref[0]).astype(BF16)
        z = jnp.dot(u, w_ref[...], preferred_element_type=F32).astype(BF16)
        z_lr = jnp.dot(u, wlr_ref[...], preferred_element_type=F32)
        lr_col = lax.broadcasted_iota(jnp.int32, z_lr.shape, 1)
        z_lr = jnp.where(lr_col < GLA_GATE_RANK, z_lr, 0.0).astype(BF16)
        pre = jnp.dot(z_lr, wg_ref[...], preferred_element_type=F32) + bg_ref[...]
        log_a = (jnp.minimum(pre, 0.0) - jnp.log1p(jnp.exp(-jnp.abs(pre)))) * (1.0 / GLA_TAU)
        halo, states = _mix_rows(z, log_a, halo, states, t * rows + r * sub, wpool_ref, pscale_ref, gng_ref,
                                 o_ref, r * sub)
    for h in range(GLA_HEADS):
        state_ref[h] = states[h]
    halo_ref[...] = halo


def _even_front(u, e, w_in, w_gate, b_gate, w_pool, pool_scale, gla_norm_g, modulation=None):
    bsz, seq, d = u.shape
    hk = w_gate.shape[2]
    rows = min(1024, seq)
    sub = min(256, rows)

    def layer(shape, *tail):
        return pl.BlockSpec((None,) + shape, lambda b, t: (e,) + (tail or (0,) * len(shape)))

    vec = pl.BlockSpec((1, 1, d), lambda b, t: (b, 0, 0))
    lead_specs = [vec, vec] if modulation else []
    vmem = (_nbytes((d, OFF_LR), BF16) + 2 * (_nbytes((rows, d), u.dtype) + _nbytes((rows, D_MODEL), BF16))
            + 2 * _nbytes((sub, OFF_LR), F32) + 24 * _nbytes((sub, POOL_WIDTH), F32) + 4 * 2**20)
    return pl.pallas_call(
        functools.partial(_even_front_kernel, sub=sub, modulate=bool(modulation)),
        grid=(bsz, seq // rows),
        in_specs=[pl.BlockSpec((1, rows, d), lambda b, t: (b, t, 0))] + lead_specs + [
            pl.BlockSpec((None, d, OFF_LR), lambda b, t: (e, 0, 0), pipeline_mode=pl.Buffered(1)),
            layer((d, LANES), 0, OFF_LR // LANES),
            layer((LANES, hk)), layer((1, hk)),
            layer((len(POOL_WINDOWS), POOL_GROUP, POOL_GROUP)), layer((1, POOL_WIDTH)),
            layer((1, GLA_HEADS * GLA_DV)),
        ],
        out_specs=pl.BlockSpec((1, rows, D_MODEL), lambda b, t: (b, t, 0)),
        out_shape=jax.ShapeDtypeStruct((bsz, seq, D_MODEL), BF16),
        scratch_shapes=[pltpu.VMEM((GLA_HEADS, GLA_DV, GLA_DK), F32), pltpu.VMEM((HALO, POOL_WIDTH), F32)],
        compiler_params=_cparams(("arbitrary", "arbitrary"), vmem),
        name="even_front",
    )(u, *(modulation or ()), w_in, w_in, w_gate, b_gate, w_pool, pool_scale, gla_norm_g)


def _attn_kernel(diag_ref, q_ref, k0_ref, k1_ref, k2_ref, v0_ref, v1_ref, v2_ref, o_ref, bias_ref):
    hb = diag_ref.shape[0]

    @pl.when((pl.program_id(1) == 0) & (pl.program_id(2) == 0))
    def _():
        qpos = lax.broadcasted_iota(jnp.int32, (ATT_Q, ATT_WIN), 0)
        kpos = lax.broadcasted_iota(jnp.int32, (ATT_Q, ATT_WIN), 1)
        lag = kpos // CHUNK - qpos // CHUNK
        in_band = (lag >= 0) & (lag <= ATT_PAST_CHUNKS)
        for h in range(hb):
            rows = jnp.broadcast_to(diag_ref[h], (ATT_Q, ATT_DIAG))
            table = pltpu.roll(rows, 0, 1, stride=1, stride_axis=0)[:, :ATT_WIN] * LOG2_E
            table = jnp.where(in_band, table, NEG_INF)
            for step in range(ATT_KEY_BLOCKS):
                lead = (ATT_KEY_BLOCKS - 1 - step) * ATT_Q
                bias_ref[step, h] = jnp.where(kpos >= lead, table, NEG_INF)

    variant = jnp.minimum(pl.program_id(2), ATT_KEY_BLOCKS - 1)
    k_refs = (k0_ref, k1_ref, k2_ref)
    v_refs = (v0_ref, v1_ref, v2_ref)
    nt = (((1,), (1,)), ((), ()))
    def head_cols(h):
        return slice(h * ATT_HEAD_DIM, (h + 1) * ATT_HEAD_DIM)

    def qk(h):
        q = q_ref[0, :, head_cols(h)]
        return jnp.concatenate(
            [lax.dot_general(q, kr[0, :, head_cols(h)], nt, preferred_element_type=F32) for kr in k_refs],
            axis=1)

    def softmax(h, s):
        s = s + bias_ref[variant, h]
        p = jnp.exp2(s - jnp.max(s, axis=-1, keepdims=True))
        return p.astype(BF16), jnp.sum(p, axis=-1, keepdims=True)

    def pv(h, p, denom):
        o = jnp.dot(p[:, 0:ATT_Q], v_refs[0][0, :, head_cols(h)], preferred_element_type=F32)
        for j in range(1, ATT_KEY_BLOCKS):
            o = o + jnp.dot(p[:, j * ATT_Q:(j + 1) * ATT_Q], v_refs[j][0, :, head_cols(h)],
                            preferred_element_type=F32)
        o_ref[0, :, head_cols(h)] = (o / denom).astype(BF16)

    scores, probs = {}, {}
    for t in range(hb + 2):
        if t < hb:
            scores[t] = qk(t)
        if 0 <= t - 1 < hb:
            probs[t - 1] = softmax(t - 1, scores.pop(t - 1))
        if 0 <= t - 2 < hb:
            pv(t - 2, *probs.pop(t - 2))


def _band_attention(qkv, diag):
    bsz, seq, _ = qkv.shape
    hb = ATT_HB
    width = hb * ATT_HEAD_DIM
    groups = ATT_HEADS // hb
    blk = (1, ATT_Q, width)

    def key_spec(col0, back):
        return pl.BlockSpec(blk, lambda g, b, i: (b, jnp.maximum(i - back, 0), col0 + g))

    vmem = (2 * 8 * _nbytes(blk, BF16) + _nbytes((ATT_KEY_BLOCKS, hb, ATT_Q, ATT_WIN), F32)
            + 8 * _nbytes((ATT_Q, ATT_WIN), F32) + 4 * 2**20)
    return pl.pallas_call(
        _attn_kernel,
        grid=(groups, bsz, seq // ATT_Q),
        in_specs=[
            pl.BlockSpec((hb, 1, ATT_DIAG), lambda g, b, i: (g, 0, 0)),
            pl.BlockSpec(blk, lambda g, b, i: (b, i, g)),
            key_spec(groups, 2), key_spec(groups, 1), key_spec(groups, 0),
            key_spec(2 * groups, 2), key_spec(2 * groups, 1), key_spec(2 * groups, 0),
        ],
        out_specs=pl.BlockSpec(blk, lambda g, b, i: (b, i, g)),
        out_shape=jax.ShapeDtypeStruct((bsz, seq, D_MODEL), BF16),
        scratch_shapes=[pltpu.VMEM((ATT_KEY_BLOCKS, hb, ATT_Q, ATT_WIN), F32)],
        compiler_params=_cparams(("arbitrary", "arbitrary", "arbitrary"), vmem),
        name="band_attention",
    )(diag, qkv, qkv, qkv, qkv, qkv, qkv, qkv)


def _attention_bias_diagonal(rel_bias):
    pad = ATT_PAST_CHUNKS * CHUNK
    m = jnp.arange(ATT_DIAG)
    q_minus_k = jnp.where(m < ATT_WIN, -m, ATT_DIAG - m)
    rel = jnp.clip(pad + q_minus_k, -REL_CLIP, REL_CLIP) + REL_CLIP
    return rel_bias[:, rel].astype(F32)[:, None, :]


def _mm_ln_kernel(a_ref, w_ref, x_ref, gate_ref, lng_ref, lnb_ref, *rest, alpha, emit_u, sub):
    if emit_u:
        sc_ref, sh_ref, xo_ref, uo_ref = rest
    else:
        (xo_ref,) = rest
    for r in range(a_ref.shape[1] // sub):
        rows = slice(r * sub, (r + 1) * sub)
        y = jnp.dot(a_ref[0, rows, :], w_ref[...], preferred_element_type=F32)
        t = alpha * x_ref[0, rows, :] + (1.0 + gate_ref[0]) * y
        mu = jnp.mean(t, axis=-1, keepdims=True)
        tc = t - mu
        var = jnp.mean(tc * tc, axis=-1, keepdims=True)
        xn = (tc * lax.rsqrt(var + LN_EPS)) * lng_ref[...] + lnb_ref[...]
        xo_ref[0, rows, :] = xn
        if emit_u:
            uo_ref[0, rows, :] = (xn * (1.0 + sc_ref[0]) + sh_ref[0]).astype(BF16)


def _mm_ln(a, w_stack, idx, x, gate, ln_g, ln_b, next_sc, next_sh, alpha, bm, name):
    bsz, seq, kdim = a.shape
    d = w_stack.shape[2]
    bm = min(bm, seq)
    sub = min(256, bm)
    emit_u = next_sc is not None
    vec = pl.BlockSpec((1, 1, d), lambda b, i: (b, 0, 0))
    par = pl.BlockSpec((1, d), lambda b, i: (0, 0))
    row = pl.BlockSpec((1, bm, d), lambda b, i: (b, i, 0))
    in_specs = [
        pl.BlockSpec((1, bm, kdim), lambda b, i: (b, i, 0)),
        pl.BlockSpec((None, kdim, d), lambda b, i: (idx, 0, 0), pipeline_mode=pl.Buffered(1)),
        row, vec, par, par,
    ]
    args = [a, w_stack, x, gate, ln_g, ln_b]
    out_specs = [row]
    out_shape = [jax.ShapeDtypeStruct((bsz, seq, d), F32)]
    if emit_u:
        in_specs += [vec, vec]
        args += [next_sc, next_sh]
        out_specs.append(row)
        out_shape.append(jax.ShapeDtypeStruct((bsz, seq, d), BF16))
    vmem = (_nbytes((kdim, d), BF16)
            + 2 * (_nbytes((bm, kdim), BF16) + 2 * _nbytes((bm, d), F32) + _nbytes((bm, d), BF16))
            + 6 * _nbytes((sub, d), F32) + 2 * 2**20)
    outs = pl.pallas_call(
        functools.partial(_mm_ln_kernel, alpha=alpha, emit_u=emit_u, sub=sub),
        grid=(bsz, seq // bm),
        in_specs=in_specs,
        out_specs=out_specs,
        out_shape=out_shape,
        compiler_params=_cparams(("arbitrary", "arbitrary"), vmem),
        name=name,
    )(*args)
    return (outs[0], outs[1]) if emit_u else (outs[0], None)


def _ffn_up_kernel(u_ref, wa_ref, wg_ref, cw_ref, cb_ref, h_ref, tail_ref, *, sub):
    @pl.when(pl.program_id(2) == 0)
    def _():
        tail_ref[...] = jnp.zeros(tail_ref.shape, F32)

    tail = tail_ref[...]
    row = lax.broadcasted_iota(jnp.int32, (sub, 1), 0)
    for r in range(u_ref.shape[1] // sub):
        rows = slice(r * sub, (r + 1) * sub)
        u = u_ref[0, rows, :]
        prev1 = tail[CONV_TAIL - 1:CONV_TAIL]
        prev2 = tail[CONV_TAIL - 2:CONV_TAIL - 1]
        g = jnp.dot(u, wg_ref[...], preferred_element_type=F32)
        g1 = jnp.where(row == 0, prev1, pltpu.roll(g, 1, 0))
        g2 = jnp.where(row == 0, prev2, jnp.where(row == 1, prev1, pltpu.roll(g, 2, 0)))
        gate = jax.nn.gelu(cb_ref[...] + g2 * cw_ref[0:1] + g1 * cw_ref[1:2] + g * cw_ref[2:3])
        a = jnp.dot(u, wa_ref[...], preferred_element_type=F32)
        h_ref[0, rows, :] = (gate * a).astype(BF16)
        tail = g[sub - CONV_TAIL:sub]
    tail_ref[...] = tail


def _ffn_up(u, layer, w_a, w_g, conv_w, conv_b):
    bsz, seq, d = u.shape
    ffn = w_g.shape[2]
    bm = min(2048, seq)
    sub = min(1024, bm)
    bn = FFN_TILE
    n_tiles = -(-ffn // bn)
    vmem = (2 * (_nbytes((bm, d), BF16) + 2 * _nbytes((d, bn), BF16) + _nbytes((bm, bn), BF16))
            + 8 * _nbytes((sub, bn), F32) + 2 * 2**20)
    return pl.pallas_call(
        functools.partial(_ffn_up_kernel, sub=sub),
        grid=(bsz, n_tiles, seq // bm),
        in_specs=[
            pl.BlockSpec((1, bm, d), lambda b, j, i: (b, i, 0)),
            pl.BlockSpec((None, d, bn), lambda b, j, i: (layer, 0, j)),
            pl.BlockSpec((None, d, bn), lambda b, j, i: (layer, 0, j)),
            pl.BlockSpec((None, 3, bn), lambda b, j, i: (layer, 0, j)),
            pl.BlockSpec((None, 1, bn), lambda b, j, i: (layer, 0, j)),
        ],
        out_specs=pl.BlockSpec((1, bm, bn), lambda b, j, i: (b, i, j)),
        out_shape=jax.ShapeDtypeStruct((bsz, seq, ffn), BF16),
        scratch_shapes=[pltpu.VMEM((CONV_TAIL, bn), F32)],
        compiler_params=_cparams(("arbitrary", "arbitrary", "arbitrary"), vmem,
                                 fuse_inputs=[False, True, False, False, False]),
        name="ffn_up",
    )(u, w_a, w_g, conv_w, conv_b)


def kernel(x, c, w_ada, b_ada, ln_g, ln_b, w_in_ab, w_gate_lr, b_gate, gla_norm_g, w_pool, pool_scale,
           w_out_ab, w_qkv, rel_bias, w_o, w_up, conv_w, conv_b, w_down):
    bsz, seq, d = x.shape
    depth = w_ada.shape[0]
    alpha = (2.0 * depth) ** 0.25

    mod = _ada_modulation(c, w_ada, b_ada).reshape(depth, bsz, 6, 1, d)

    def mod_vec(layer, idx):
        return mod[layer, :, idx]

    w_in = w_in_ab.astype(BF16)
    w_gate = jnp.pad(w_gate_lr, ((0, 0), (0, LANES - GLA_GATE_RANK), (0, 0))).astype(BF16)
    w_pool_b = w_pool.astype(BF16)
    w_out = w_out_ab.astype(BF16)
    col_scale = jnp.where(jnp.arange(3 * d) < d, ATT_HEAD_DIM ** -0.5 * LOG2_E, 1.0).astype(F32)
    w_att = (w_qkv * col_scale[None, None, :]).astype(BF16)
    w_o_b = w_o.astype(BF16)
    w_up_b = w_up.astype(BF16)
    w_g = w_up[:, :, FFN_DIM:].astype(BF16)
    w_down_b = w_down.astype(BF16)

    u = None
    for layer in range(depth):
        half = layer // 2
        if layer % 2 == 0:
            first = (x, (mod_vec(0, 1), mod_vec(0, 0))) if layer == 0 else (u, None)
            y = _even_front(first[0], half, w_in, w_gate, b_gate[:, None, :], w_pool_b,
                            pool_scale[:, None, :], gla_norm_g[:, None, :], modulation=first[1])
            w_proj = w_out
        else:
            qkv = _matmul(u.reshape(bsz * seq, d), w_att, half, "qkv_proj").reshape(bsz, seq, 3 * d)
            y = _band_attention(qkv, _attention_bias_diagonal(rel_bias[half]))
            w_proj = w_o_b
        x, u = _mm_ln(y, w_proj, half, x, mod_vec(layer, 2), ln_g[layer, 0][None], ln_b[layer, 0][None],
                      mod_vec(layer, 4), mod_vec(layer, 3), alpha, 512, "mixer_out_ln")
        h = _ffn_up(u, layer, w_up_b, w_g, conv_w, conv_b[:, None, :])
        last = layer == depth - 1
        x, u = _mm_ln(h, w_down_b, layer, x, mod_vec(layer, 5), ln_g[layer, 1][None], ln_b[layer, 1][None],
                      None if last else mod_vec(layer + 1, 1), None if last else mod_vec(layer + 1, 0),
                      alpha, 512, "ffn_down_ln")
    return x
```

```python
import functools

import jax
import jax.numpy as jnp
from jax import lax
from jax.experimental import pallas as pl
from jax.experimental.pallas import tpu as pltpu

F32 = jnp.float32
BF16 = jnp.bfloat16

D_MODEL = 2048
CHUNK = 64
POOL_WIDTH = D_MODEL // 2
POOL_WINDOWS = (2, 4, 8, 16)
POOL_GROUP = POOL_WIDTH // len(POOL_WINDOWS)
GLA_HEADS = 4
GLA_DV = (D_MODEL // 2) // GLA_HEADS
GLA_DK = GLA_DV // 2
GLA_GATE_RANK = 16
GLA_TAU = 16.0
OFF_Q = POOL_WIDTH
OFF_K = OFF_Q + GLA_HEADS * GLA_DK
OFF_V = OFF_K + GLA_HEADS * GLA_DK
OFF_R = OFF_V + GLA_HEADS * GLA_DV
OFF_LR = OFF_R + GLA_HEADS * GLA_DV
ATT_HEADS = 16
ATT_HEAD_DIM = D_MODEL // ATT_HEADS
ATT_PAST_CHUNKS = 8
REL_CLIP = 256
FFN_DIM = 5504
LN_EPS = 1e-5
NEG_INF = -1e30

V7X_VMEM_BYTES = 64 * 2**20
LANES = 128
F32_SUBLANES = 8
BF16_SUBLANES = 16
CONV_TAIL = F32_SUBLANES

FFN_TILE = 512
HALO = BF16_SUBLANES
ATT_Q = 4 * CHUNK
ATT_WIN = ATT_Q + ATT_PAST_CHUNKS * CHUNK
ATT_HB = 8
ATT_KEY_BLOCKS = ATT_WIN // ATT_Q
ATT_DIAG = ATT_WIN + ATT_Q
LOG2_E = 1.4426950408889634


def _cparams(semantics, vmem_bytes):
    limit = min(int(vmem_bytes), V7X_VMEM_BYTES - 4 * 2**20)
    return pltpu.CompilerParams(dimension_semantics=semantics, vmem_limit_bytes=limit)


def _nbytes(shape, dtype):
    n = 1
    for s in shape:
        n *= s
    return n * jnp.dtype(dtype).itemsize


def _ada_kernel(c_ref, w_ref, b_ref, o_ref):
    c = c_ref[...]
    cond = (c * jax.nn.sigmoid(c)).astype(BF16)
    o_ref[0] = jnp.dot(cond, w_ref[0].astype(BF16), preferred_element_type=F32) + b_ref[0]


def _ada_modulation(c, w_ada, b_ada):
    depth, d, n = w_ada.shape
    bsz = c.shape[0]
    rows = -(-bsz // 8) * 8
    bn = 1024
    c_pad = jnp.pad(c, ((0, rows - bsz), (0, 0)))
    vmem = 2 * (_nbytes((d, bn), F32) + _nbytes((rows, bn), F32)) + _nbytes((d, bn), BF16) + 2**20
    out = pl.pallas_call(
        _ada_kernel,
        grid=(depth, n // bn),
        in_specs=[
            pl.BlockSpec((rows, d), lambda l, j: (0, 0)),
            pl.BlockSpec((1, d, bn), lambda l, j: (l, 0, j)),
            pl.BlockSpec((1, 1, bn), lambda l, j: (l, 0, j)),
        ],
        out_specs=pl.BlockSpec((1, rows, bn), lambda l, j: (l, 0, j)),
        out_shape=jax.ShapeDtypeStruct((depth, rows, n), F32),
        compiler_params=_cparams(("arbitrary", "arbitrary"), vmem),
        name="ada_modulation",
    )(c_pad, w_ada, b_ada.reshape(depth, 1, n))
    return out[:, :bsz]


def _matmul_kernel(a_ref, w_ref, o_ref):
    o_ref[...] = jnp.dot(a_ref[...], w_ref[...], preferred_element_type=F32).astype(o_ref.dtype)


def _matmul(a, w_stack, idx, name):
    m, k = a.shape
    n = w_stack.shape[2]
    bm = min(1024, m)
    bn = min(2048, n)
    vmem = (2 * (_nbytes((bm, k), BF16) + _nbytes((k, bn), BF16) + _nbytes((bm, bn), BF16))
            + _nbytes((bm, bn), F32) + 2**20)
    return pl.pallas_call(
        _matmul_kernel,
        grid=(m // bm, n // bn),
        in_specs=[
            pl.BlockSpec((bm, k), lambda i, j: (i, 0)),
            pl.BlockSpec((None, k, bn), lambda i, j: (idx, 0, j)),
        ],
        out_specs=pl.BlockSpec((bm, bn), lambda i, j: (i, j)),
        out_shape=jax.ShapeDtypeStruct((m, n), BF16),
        compiler_params=_cparams(("arbitrary", "arbitrary"), vmem),
        name=name,
    )(a, w_stack)


def _mix_rows(z, log_a, halo, states, first_row, wpool_ref, pscale_ref, gng_ref, o_ref, out_row):
    rows = z.shape[0]
    n_chunks = rows // CHUNK

    pooled = z[:, 0:POOL_WIDTH].astype(F32)
    ext = jnp.concatenate([halo, pooled], axis=0)
    grow = lax.broadcasted_iota(jnp.int32, (rows, 1), 0) + first_row
    for gi, w in enumerate(POOL_WINDOWS):
        cols = slice(gi * POOL_GROUP, (gi + 1) * POOL_GROUP)
        e = ext[:, cols]
        s = e
        step = 1
        while step < w:
            s = s + pltpu.roll(s, step, 0)
            step *= 2
        count = jnp.minimum(grow + 1, w).astype(F32)
        dev = s[HALO:] / count - e[HALO:]
        y_a = jnp.dot(dev.astype(BF16), wpool_ref[gi], preferred_element_type=F32) * pscale_ref[:, cols]
        o_ref[0, out_row:out_row + rows, cols] = y_a.astype(BF16)

    in_chunk = lax.broadcasted_iota(jnp.int32, (rows, 1), 0) % CHUNK
    b_all = log_a
    step = 1
    while step < CHUNK:
        b_all = b_all + jnp.where(in_chunk >= step, pltpu.roll(b_all, step, 0), 0.0)
        step *= 2
    q_all = z[:, OFF_Q:OFF_K].astype(F32)
    k_all = z[:, OFF_K:OFF_V].astype(F32)
    q_dec = ((q_all * (GLA_DK ** -0.5)) * jnp.exp(b_all)).astype(BF16)
    k_inv = (k_all * jnp.exp(-b_all)).astype(BF16)
    causal = (lax.broadcasted_iota(jnp.int32, (CHUNK, CHUNK), 0)
              >= lax.broadcasted_iota(jnp.int32, (CHUNK, CHUNK), 1))
    nt = (((1,), (1,)), ((), ()))
    units = [(c, h) for c in range(n_chunks) for h in range(GLA_HEADS)]

    def cut(c, h):
        return (slice(c * CHUNK, (c + 1) * CHUNK), slice(h * GLA_DK, (h + 1) * GLA_DK),
                slice(h * GLA_DV, (h + 1) * GLA_DV))

    intra, update, decay = {}, {}, {}
    for c in range(n_chunks):
        rs = slice(c * CHUNK, (c + 1) * CHUNK)
        b_c = b_all[rs]
        b_last = b_c[CHUNK - 1:CHUNK]
        k_dec = (k_all[rs] * jnp.exp(b_last - b_c)).astype(BF16)
        decay_c = jnp.exp(b_last)
        for h in range(GLA_HEADS):
            _, ks, vs = cut(c, h)
            vh = z[rs, OFF_V + h * GLA_DV:OFF_V + (h + 1) * GLA_DV]
            scores = lax.dot_general(q_dec[rs, ks], k_inv[rs, ks], nt, preferred_element_type=F32)
            scores = jnp.where(causal, scores, 0.0).astype(BF16)
            intra[c, h] = jnp.dot(scores, vh, preferred_element_type=F32)
            update[c, h] = lax.dot_general(vh, k_dec[:, ks], (((0,), (0,)), ((), ())),
                                           preferred_element_type=F32)
            decay[c, h] = decay_c[:, ks]
    before = {}
    new_states = []
    for h in range(GLA_HEADS):
        state = states[h]
        for c in range(n_chunks):
            before[c, h] = state.astype(BF16)
            state = decay[c, h] * state + update[c, h]
        new_states.append(state)
    for c, h in units:
        rs, ks, vs = cut(c, h)
        o = intra[c, h] + lax.dot_general(q_dec[rs, ks], before[c, h], nt, preferred_element_type=F32)
        mu = jnp.mean(o, axis=-1, keepdims=True)
        oc = o - mu
        var = jnp.mean(oc * oc, axis=-1, keepdims=True)
        o_n = (oc * lax.rsqrt(var + LN_EPS)) * gng_ref[:, vs]
        r_h = z[rs, OFF_R + h * GLA_DV:OFF_R + (h + 1) * GLA_DV].astype(F32)
        y_b = (r_h * jax.nn.sigmoid(r_h)) * o_n
        o_ref[0, out_row + c * CHUNK:out_row + (c + 1) * CHUNK,
              POOL_WIDTH + h * GLA_DV:POOL_WIDTH + (h + 1) * GLA_DV] = y_b.astype(BF16)
    return pooled[rows - HALO:rows], new_states


def _even_front_kernel(*refs, sub, modulate):
    if modulate:
        sc_ref, sh_ref = refs[1:3]
        refs = refs[:1] + refs[3:]
    u_ref, w_ref, wlr_ref, wg_ref, bg_ref, wpool_ref, pscale_ref, gng_ref, o_ref, state_ref, halo_ref = refs
    t = pl.program_id(1)
    rows = u_ref.shape[1]

    @pl.when(t == 0)
    def _():
        state_ref[...] = jnp.zeros(state_ref.shape, F32)
        halo_ref[...] = jnp.zeros(halo_ref.shape, F32)

    states = [state_ref[h] for h in range(GLA_HEADS)]
    halo = halo_ref[...]
    for r in range(rows // sub):
        u = u_ref[0, r * sub:(r + 1) * sub, :]
        if modulate:
            u = (u * (1.0 + sc_ref[0]) + sh_ref[0]).astype(BF16)
        z = jnp.dot(u, w_ref[...], preferred_element_type=F32).astype(BF16)
        z_lr = jnp.dot(u, wlr_ref[...], preferred_element_type=F32)
        lr_col = lax.broadcasted_iota(jnp.int32, z_lr.shape, 1)
        z_lr = jnp.where(lr_col < GLA_GATE_RANK, z_lr, 0.0).astype(BF16)
        pre = jnp.dot(z_lr, wg_ref[...], preferred_element_type=F32) + bg_ref[...]
        log_a = (jnp.minimum(pre, 0.0) - jnp.log1p(jnp.exp(-jnp.abs(pre)))) * (1.0 / GLA_TAU)
        halo, states = _mix_rows(z, log_a, halo, states, t * rows + r * sub, wpool_ref, pscale_ref, gng_ref,
                                 o_ref, r * sub)
    for h in range(GLA_HEADS):
        state_ref[h] = states[h]
    halo_ref[...] = halo


def _even_front(u, e, w_in, w_gate, b_gate, w_pool, pool_scale, gla_norm_g, modulation=None):
    bsz, seq, d = u.shape
    hk = w_gate.shape[2]
    rows = min(1024, seq)
    sub = min(256, rows)

    def layer(shape, *tail):
        return pl.BlockSpec((None,) + shape, lambda b, t: (e,) + (tail or (0,) * len(shape)))

    vec = pl.BlockSpec((1, 1, d), lambda b, t: (b, 0, 0))
    lead_specs = [vec, vec] if modulation else []
    vmem = (_nbytes((d, OFF_LR), BF16) + 2 * (_nbytes((rows, d), u.dtype) + _nbytes((rows, D_MODEL), BF16))
            + 2 * _nbytes((sub, OFF_LR), F32) + 24 * _nbytes((sub, POOL_WIDTH), F32) + 4 * 2**20)
    return pl.pallas_call(
        functools.partial(_even_front_kernel, sub=sub, modulate=bool(modulation)),
        grid=(bsz, seq // rows),
        in_specs=[pl.BlockSpec((1, rows, d), lambda b, t: (b, t, 0))] + lead_specs + [
            pl.BlockSpec((None, d, OFF_LR), lambda b, t: (e, 0, 0), pipeline_mode=pl.Buffered(1)),
            layer((d, LANES), 0, OFF_LR // LANES),
            layer((LANES, hk)), layer((1, hk)),
            layer((len(POOL_WINDOWS), POOL_GROUP, POOL_GROUP)), layer((1, POOL_WIDTH)),
            layer((1, GLA_HEADS * GLA_DV)),
        ],
        out_specs=pl.BlockSpec((1, rows, D_MODEL), lambda b, t: (b, t, 0)),
        out_shape=jax.ShapeDtypeStruct((bsz, seq, D_MODEL), BF16),
        scratch_shapes=[pltpu.VMEM((GLA_HEADS, GLA_DV, GLA_DK), F32), pltpu.VMEM((HALO, POOL_WIDTH), F32)],
        compiler_params=_cparams(("arbitrary", "arbitrary"), vmem),
        name="even_front",
    )(u, *(modulation or ()), w_in, w_in, w_gate, b_gate, w_pool, pool_scale, gla_norm_g)


def _attn_kernel(diag_ref, q_ref, k0_ref, k1_ref, k2_ref, v0_ref, v1_ref, v2_ref, o_ref, bias_ref):
    hb = diag_ref.shape[0]

    @pl.when((pl.program_id(1) == 0) & (pl.program_id(2) == 0))
    def _():
        qpos = lax.broadcasted_iota(jnp.int32, (ATT_Q, ATT_WIN), 0)
        kpos = lax.broadcasted_iota(jnp.int32, (ATT_Q, ATT_WIN), 1)
        lag = kpos // CHUNK - qpos // CHUNK
        in_band = (lag >= 0) & (lag <= ATT_PAST_CHUNKS)
        for h in range(hb):
            rows = jnp.broadcast_to(diag_ref[h], (ATT_Q, ATT_DIAG))
            table = pltpu.roll(rows, 0, 1, stride=1, stride_axis=0)[:, :ATT_WIN] * LOG2_E
            table = jnp.where(in_band, table, NEG_INF)
            for step in range(ATT_KEY_BLOCKS):
                lead = (ATT_KEY_BLOCKS - 1 - step) * ATT_Q
                bias_ref[step, h] = jnp.where(kpos >= lead, table, NEG_INF)

    variant = jnp.minimum(pl.program_id(2), ATT_KEY_BLOCKS - 1)
    k_refs = (k0_ref, k1_ref, k2_ref)
    v_refs = (v0_ref, v1_ref, v2_ref)
    nt = (((1,), (1,)), ((), ()))
    def head_cols(h):
        return slice(h * ATT_HEAD_DIM, (h + 1) * ATT_HEAD_DIM)

    def qk(h):
        q = q_ref[0, :, head_cols(h)]
        return jnp.concatenate(
            [lax.dot_general(q, kr[0, :, head_cols(h)], nt, preferred_element_type=F32) for kr in k_refs],
            axis=1)

    def softmax(h, s):
        s = s + bias_ref[variant, h]
        p = jnp.exp2(s - jnp.max(s, axis=-1, keepdims=True))
        return p.astype(BF16), jnp.sum(p, axis=-1, keepdims=True)

    def pv(h, p, denom):
        o = jnp.dot(p[:, 0:ATT_Q], v_refs[0][0, :, head_cols(h)], preferred_element_type=F32)
        for j in range(1, ATT_KEY_BLOCKS):
            o = o + jnp.dot(p[:, j * ATT_Q:(j + 1) * ATT_Q], v_refs[j][0, :, head_cols(h)],
                            preferred_element_type=F32)
        o_ref[0, :, head_cols(h)] = (o / denom).astype(BF16)

    scores, probs = {}, {}
    for t in range(hb + 2):
        if t < hb:
            scores[t] = qk(t)
        if 0 <= t - 1 < hb:
            probs[t - 1] = softmax(t - 1, scores.pop(t - 1))
        if 0 <= t - 2 < hb:
            pv(t - 2, *probs.pop(t - 2))


def _band_attention(qkv, diag):
    bsz, seq, _ = qkv.shape
    hb = ATT_HB
    width = hb * ATT_HEAD_DIM
    groups = ATT_HEADS // hb
    blk = (1, ATT_Q, width)

    def key_spec(col0, back):
        return pl.BlockSpec(blk, lambda g, b, i: (b, jnp.maximum(i - back, 0), col0 + g))

    vmem = (2 * 8 * _nbytes(blk, BF16) + _nbytes((ATT_KEY_BLOCKS, hb, ATT_Q, ATT_WIN), F32)
            + 8 * _nbytes((ATT_Q, ATT_WIN), F32) + 4 * 2**20)
    return pl.pallas_call(
        _attn_kernel,
        grid=(groups, bsz, seq // ATT_Q),
        in_specs=[
            pl.BlockSpec((hb, 1, ATT_DIAG), lambda g, b, i: (g, 0, 0)),
            pl.BlockSpec(blk, lambda g, b, i: (b, i, g)),
            key_spec(groups, 2), key_spec(groups, 1), key_spec(groups, 0),
            key_spec(2 * groups, 2), key_spec(2 * groups, 1), key_spec(2 * groups, 0),
        ],
        out_specs=pl.BlockSpec(blk, lambda g, b, i: (b, i, g)),
        out_shape=jax.ShapeDtypeStruct((bsz, seq, D_MODEL), BF16),
        scratch_shapes=[pltpu.VMEM((ATT_KEY_BLOCKS, hb, ATT_Q, ATT_WIN), F32)],
        compiler_params=_cparams(("arbitrary", "arbitrary", "arbitrary"), vmem),
        name="band_attention",
    )(diag, qkv, qkv, qkv, qkv, qkv, qkv, qkv)


def _attention_bias_diagonal(rel_bias):
    pad = ATT_PAST_CHUNKS * CHUNK
    m = jnp.arange(ATT_DIAG)
    q_minus_k = jnp.where(m < ATT_WIN, -m, ATT_DIAG - m)
    rel = jnp.clip(pad + q_minus_k, -REL_CLIP, REL_CLIP) + REL_CLIP
    return rel_bias[:, rel].astype(F32)[:, None, :]


def _mm_ln_kernel(a_ref, w_ref, x_ref, gate_ref, lng_ref, lnb_ref, *rest, alpha, emit_u, sub):
    if emit_u:
        sc_ref, sh_ref, xo_ref, uo_ref = rest
    else:
        (xo_ref,) = rest
    for r in range(a_ref.shape[1] // sub):
        rows = slice(r * sub, (r + 1) * sub)
        y = jnp.dot(a_ref[0, rows, :], w_ref[...], preferred_element_type=F32)
        t = alpha * x_ref[0, rows, :] + (1.0 + gate_ref[0]) * y
        mu = jnp.mean(t, axis=-1, keepdims=True)
        var = jnp.mean(t * t, axis=-1, keepdims=True) - mu * mu
        xn = ((t - mu) * lax.rsqrt(var + LN_EPS)) * lng_ref[...] + lnb_ref[...]
        xo_ref[0, rows, :] = xn
        if emit_u:
            uo_ref[0, rows, :] = (xn * (1.0 + sc_ref[0]) + sh_ref[0]).astype(BF16)


def _mm_ln(a, w_stack, idx, x, gate, ln_g, ln_b, next_sc, next_sh, alpha, bm, name):
    bsz, seq, kdim = a.shape
    d = w_stack.shape[2]
    bm = min(bm, seq)
    sub = min(256, bm)
    emit_u = next_sc is not None
    vec = pl.BlockSpec((1, 1, d), lambda b, i: (b, 0, 0))
    par = pl.BlockSpec((1, d), lambda b, i: (0, 0))
    row = pl.BlockSpec((1, bm, d), lambda b, i: (b, i, 0))
    in_specs = [
        pl.BlockSpec((1, bm, kdim), lambda b, i: (b, i, 0)),
        pl.BlockSpec((None, kdim, d), lambda b, i: (idx, 0, 0), pipeline_mode=pl.Buffered(1)),
        row, vec, par, par,
    ]
    args = [a, w_stack, x, gate, ln_g, ln_b]
    out_specs = [row]
    out_shape = [jax.ShapeDtypeStruct((bsz, seq, d), F32)]
    if emit_u:
        in_specs += [vec, vec]
        args += [next_sc, next_sh]
        out_specs.append(row)
        out_shape.append(jax.ShapeDtypeStruct((bsz, seq, d), BF16))
    vmem = (_nbytes((kdim, d), BF16)
            + 2 * (_nbytes((bm, kdim), BF16) + 2 * _nbytes((bm, d), F32) + _nbytes((bm, d), BF16))
            + 6 * _nbytes((sub, d), F32) + 2 * 2**20)
    outs = pl.pallas_call(
        functools.partial(_mm_ln_kernel, alpha=alpha, emit_u=emit_u, sub=sub),
        grid=(bsz, seq // bm),
        in_specs=in_specs,
        out_specs=out_specs,
        out_shape=out_shape,
        compiler_params=_cparams(("arbitrary", "arbitrary"), vmem),
        name=name,
    )(*args)
    return (outs[0], outs[1]) if emit_u else (outs[0], None)


def _ffn_up_kernel(u_ref, wa_ref, wg_ref, cw_ref, cb_ref, h_ref, tail_ref, *, sub):
    @pl.when(pl.program_id(2) == 0)
    def _():
        tail_ref[...] = jnp.zeros(tail_ref.shape, F32)

    tail = tail_ref[...]
    row = lax.broadcasted_iota(jnp.int32, (sub, 1), 0)
    for r in range(u_ref.shape[1] // sub):
        rows = slice(r * sub, (r + 1) * sub)
        u = u_ref[0, rows, :]
        prev1 = tail[CONV_TAIL - 1:CONV_TAIL]
        prev2 = tail[CONV_TAIL - 2:CONV_TAIL - 1]
        g = jnp.dot(u, wg_ref[...], preferred_element_type=F32)
        g1 = jnp.where(row == 0, prev1, pltpu.roll(g, 1, 0))
        g2 = jnp.where(row == 0, prev2, jnp.where(row == 1, prev1, pltpu.roll(g, 2, 0)))
        gate = jax.nn.gelu(cb_ref[...] + g2 * cw_ref[0:1] + g1 * cw_ref[1:2] + g * cw_ref[2:3])
        a = jnp.dot(u, wa_ref[...], preferred_element_type=F32)
        h_ref[0, rows, :] = (gate * a).astype(BF16)
        tail = g[sub - CONV_TAIL:sub]
    tail_ref[...] = tail


def _ffn_up(u, layer, w_a, w_g, conv_w, conv_b):
    bsz, seq, d = u.shape
    ffn = w_g.shape[2]
    bm = min(2048, seq)
    sub = min(1024, bm)
    bn = FFN_TILE
    n_tiles = -(-ffn // bn)
    vmem = (2 * (_nbytes((bm, d), BF16) + 2 * _nbytes((d, bn), BF16) + _nbytes((bm, bn), BF16))
            + 8 * _nbytes((sub, bn), F32) + 2 * 2**20)
    return pl.pallas_call(
        functools.partial(_ffn_up_kernel, sub=sub),
        grid=(bsz, n_tiles, seq // bm),
        in_specs=[
            pl.BlockSpec((1, bm, d), lambda b, j, i: (b, i, 0)),
            pl.BlockSpec((None, d, bn), lambda b, j, i: (layer, 0, j)),
            pl.BlockSpec((None, d, bn), lambda b, j, i: (layer, 0, j)),
            pl.BlockSpec((None, 3, bn), lambda b, j, i: (layer, 0, j)),
            pl.BlockSpec((None, 1, bn), lambda b, j, i: (layer, 0, j)),
        ],
        out_specs=pl.BlockSpec((1, bm, bn), lambda b, j, i: (b, i, j)),
        out_shape=jax.ShapeDtypeStruct((bsz, seq, ffn), BF16),
        scratch_shapes=[pltpu.VMEM((CONV_TAIL, bn), F32)],
        compiler_params=_cparams(("arbitrary", "arbitrary", "arbitrary"), vmem),
        name="ffn_up",
    )(u, w_a, w_g, conv_w, conv_b)


def kernel(x, c, w_ada, b_ada, ln_g, ln_b, w_in_ab, w_gate_lr, b_gate, gla_norm_g, w_pool, pool_scale,
           w_out_ab, w_qkv, rel_bias, w_o, w_up, conv_w, conv_b, w_down):
    bsz, seq, d = x.shape
    depth = w_ada.shape[0]
    alpha = (2.0 * depth) ** 0.25

    mod = _ada_modulation(c, w_ada, b_ada).reshape(depth, bsz, 6, 1, d)

    def mod_vec(layer, idx):
        return mod[layer, :, idx]

    w_in = w_in_ab.astype(BF16)
    w_gate = jnp.pad(w_gate_lr, ((0, 0), (0, LANES - GLA_GATE_RANK), (0, 0))).astype(BF16)
    w_pool_b = w_pool.astype(BF16)
    w_out = w_out_ab.astype(BF16)
    col_scale = jnp.where(jnp.arange(3 * d) < d, ATT_HEAD_DIM ** -0.5 * LOG2_E, 1.0).astype(F32)
    w_att = (w_qkv * col_scale[None, None, :]).astype(BF16)
    w_o_b = w_o.astype(BF16)
    w_up_b = w_up.astype(BF16)
    w_g = w_up_b[:, :, FFN_DIM:]
    w_down_b = w_down.astype(BF16)

    u = None
    for layer in range(depth):
        half = layer // 2
        if layer % 2 == 0:
            first = (x, (mod_vec(0, 1), mod_vec(0, 0))) if layer == 0 else (u, None)
            y = _even_front(first[0], half, w_in, w_gate, b_gate[:, None, :], w_pool_b,
                            pool_scale[:, None, :], gla_norm_g[:, None, :], modulation=first[1])
            w_proj = w_out
        else:
            qkv = _matmul(u.reshape(bsz * seq, d), w_att, half, "qkv_proj").reshape(bsz, seq, 3 * d)
            y = _band_attention(qkv, _attention_bias_diagonal(rel_bias[half]))
            w_proj = w_o_b
        x, u = _mm_ln(y, w_proj, half, x, mod_vec(layer, 2), ln_g[layer, 0][None], ln_b[layer, 0][None],
                      mod_vec(layer, 4), mod_vec(layer, 3), alpha, 512, "mixer_out_ln")
        h = _ffn_up(u, layer, w_up_b, w_g, conv_w, conv_b[:, None, :])
        last = layer == depth - 1
        x, u = _mm_ln(h, w_down_b, layer, x, mod_vec(layer, 5), ln_g[layer, 1][None], ln_b[layer, 1][None],
                      None if last else mod_vec(layer + 1, 1), None if last else mod_vec(layer + 1, 0),
                      alpha, 512, "ffn_down_ln")
    return x
```
